```python
import math
import jax, jax.numpy as jnp
from jax import lax
import numpy as np

D_MODEL = 1024
BATCH = 2
SEQ = 8192
DEPTH = 1
DEC_BATCH = 32
DEC_SEQ = 1
PAST_LEN = 8192
PAGE_SIZE = 128

GLA_HEADS = 4
GLA_DK = D_MODEL // 16
GLA_DV = D_MODEL // 8
GLA_RANK = 16
GLA_TAU = 16.0
GLA_CHUNK = 16
DIFF_HEADS = 4
DIFF_HEAD_DIM = D_MODEL // 16
DIFF_V_DIM = 2 * DIFF_HEAD_DIM
Q_BLOCK = 128
GLA_QK = GLA_HEADS * GLA_DK
GLA_V = GLA_HEADS * GLA_DV
DIFF_QK = DIFF_HEADS * 2 * DIFF_HEAD_DIM
DIFF_V = DIFF_HEADS * DIFF_V_DIM
D_IN = 2 * GLA_QK + 2 * GLA_V + GLA_RANK + 2 * DIFF_QK + DIFF_V
D_MIX = GLA_V + DIFF_V
N_GROUPS = 4
EXPERTS_PER_GROUP = 4
N_EXPERTS = N_GROUPS * EXPERTS_PER_GROUP
EXPERT_FF = D_MODEL // 4
TOP_K_IN_GROUP = 2
EPS = 1e-6

kernel_name = 'hymba_gla_diffattn_hier_moe_step'


def rmsnorm(x, g):
    xf = x.astype(jnp.float32)
    y = xf * lax.rsqrt(jnp.mean(xf * xf, axis=-1, keepdims=True) + EPS)
    return (y * g.astype(jnp.float32)).astype(x.dtype)


def project_heads(h, w_in, w_a2, b_a):
    B, L, _ = h.shape
    z = h @ w_in
    sizes = (GLA_QK, GLA_QK, GLA_V, GLA_V, GLA_RANK, DIFF_QK, DIFF_QK, DIFF_V)
    idx = [int(i) for i in np.cumsum(sizes)[:-1]]
    gq, gk, gv, gr, g_lr, dq, dk, dv = jnp.split(z, idx, axis=-1)
    log_a = jax.nn.log_sigmoid((g_lr @ w_a2 + b_a).astype(jnp.float32)) / GLA_TAU
    return (gq.reshape(B, L, GLA_HEADS, GLA_DK), gk.reshape(B, L, GLA_HEADS, GLA_DK),
            gv.reshape(B, L, GLA_HEADS, GLA_DV), gr,
            log_a.reshape(B, L, GLA_HEADS, GLA_DK),
            dq.reshape(B, L, DIFF_HEADS, 2 * DIFF_HEAD_DIM),
            dk.reshape(B, L, DIFF_HEADS, 2 * DIFF_HEAD_DIM),
            dv.reshape(B, L, DIFF_HEADS, DIFF_V_DIM))


def gla_chunked(q, k, v, log_a, s0):
    B, L, H, DK = q.shape
    DV = v.shape[-1]
    C = GLA_CHUNK
    pad = (-L) % C
    NC = (L + pad) // C

    def blocks(t):
        t = jnp.pad(t.astype(jnp.float32), ((0, 0), (0, pad), (0, 0), (0, 0)))
        return t.reshape(B, NC, C, H, t.shape[-1]).transpose(0, 3, 1, 2, 4)

    qc = blocks(q) * (GLA_DK ** -0.5)
    kc = blocks(k)
    vc = blocks(v)
    bc = jnp.cumsum(blocks(log_a), axis=3)
    causal = jnp.tril(jnp.ones((C, C), dtype=bool))[:, :, None]
    rel = bc[..., :, None, :] - bc[..., None, :, :]
    decay = jnp.where(causal, jnp.exp(jnp.where(causal, rel, 0.0)), 0.0)
    scores = jnp.einsum('bhnid,bhnjd,bhnijd->bhnij', qc, kc, decay)
    o_intra = jnp.einsum('bhnij,bhnjv->bhniv', scores, vc)
    b_last = bc[..., -1:, :]
    u = jnp.einsum('bhncd,bhncv->nbhdv', kc * jnp.exp(b_last - bc), vc)
    a_chunk = jnp.exp(b_last[..., 0, :]).transpose(2, 0, 1, 3)

    def step(s, inp):
        a, uu = inp
        return a[..., None] * s + uu, s

    s_final, s_start = lax.scan(step, s0.astype(jnp.float32), (a_chunk, u))
    o_inter = jnp.einsum('bhncd,nbhdv->bhncv', qc * jnp.exp(bc), s_start)
    o = (o_intra + o_inter).transpose(0, 2, 3, 1, 4).reshape(B, NC * C, H, DV)[:, :L]
    return o.astype(v.dtype), s_final.astype(s0.dtype)


def diff_attend(q, k, v, lam, q_pos, k_pos):
    q1, q2 = jnp.split(q, 2, axis=-1)
    k1, k2 = jnp.split(k, 2, axis=-1)
    mask = k_pos[None, :] <= q_pos[:, None]
    scale = DIFF_HEAD_DIM ** -0.5

    def probs(qa, ka):
        s = jnp.einsum('bqhd,bkhd->bhqk', qa, ka).astype(jnp.float32) * scale
        return jax.nn.softmax(jnp.where(mask, s, -jnp.inf), axis=-1)

    w = probs(q1, k1) - lam * probs(q2, k2)
    return jnp.einsum('bhqk,bkhv->bqhv', w.astype(v.dtype), v)


def diff_attn_prompt(q, k, v, lam):
    B, L, H, E = q.shape
    nb = L // Q_BLOCK
    qb = q.reshape(B, nb, Q_BLOCK, H, E).transpose(1, 0, 2, 3, 4)
    k_pos = jnp.arange(L)

    def one(args):
        i, qi = args
        return diff_attend(qi, k, v, lam, i * Q_BLOCK + jnp.arange(Q_BLOCK), k_pos)

    o = lax.map(one, (jnp.arange(nb), qb))
    return o.transpose(1, 0, 2, 3, 4).reshape(B, L, H, -1)


def hier_moe(h, w_rg, b_rg, w_re, b_re, w_gate, w_up, w_down):
    B, L, D = h.shape
    t = h.reshape(B * L, D)
    lg = (t @ w_rg).astype(jnp.float32) + b_rg
    pg = jax.nn.softmax(lg, axis=-1)
    g = jnp.argmax(lg, axis=-1)
    p_top = jnp.take_along_axis(pg, g[:, None], axis=-1)
    le_all = jnp.einsum('td,gde->tge', t, w_re)
    le = jnp.take_along_axis(le_all, g[:, None, None], axis=1)[:, 0].astype(jnp.float32) + b_re[g]
    top_v, top_i = lax.top_k(le, TOP_K_IN_GROUP)
    w2 = jax.nn.softmax(top_v, axis=-1) * p_top
    ids = g[:, None] * EXPERTS_PER_GROUP + top_i
    gates = jnp.sum(jax.nn.one_hot(ids, N_EXPERTS, dtype=jnp.float32) * w2[..., None], axis=1)
    hid = jax.nn.silu(jnp.einsum('td,edf->tef', t, w_gate)) * jnp.einsum('td,edf->tef', t, w_up)
    hid = hid * gates[..., None].astype(hid.dtype)
    out = jnp.einsum('tef,efd->td', hid, w_down)
    return out.reshape(B, L, D)


def hybrid_layer(x, p, lam, lam_init, s0, past_k, past_v):
    (norm1, w_in, w_a2, b_a, gla_g, diff_g, w_out, norm2,
     w_rg, b_rg, w_re, b_re, w_gate, w_up, w_down) = p
    B, L, _ = x.shape
    h = rmsnorm(x, norm1)
    gq, gk, gv, gr, log_a, dq, dk, dv = project_heads(h, w_in, w_a2, b_a)
    gla_o, s_new = gla_chunked(gq, gk, gv, log_a, s0)
    gla_o = rmsnorm(gla_o, gla_g) * jax.nn.silu(gr).reshape(B, L, GLA_HEADS, GLA_DV)
    if past_k is None:
        diff_o = diff_attn_prompt(dq, dk, dv, lam)
    else:
        past_len = past_k.shape[1]
        keys = jnp.concatenate([past_k, dk], axis=1)
        vals = jnp.concatenate([past_v, dv], axis=1)
        diff_o = diff_attend(dq, keys, vals, lam, past_len + jnp.arange(L), jnp.arange(past_len + L))
    diff_o = rmsnorm(diff_o, diff_g) * (1.0 - lam_init)
    mix = jnp.concatenate([gla_o.reshape(B, L, GLA_V), diff_o.reshape(B, L, DIFF_V)], axis=-1)
    x = x + mix @ w_out
    x = x + hier_moe(rmsnorm(x, norm2), w_rg, b_rg, w_re, b_re, w_gate, w_up, w_down)
    return x, dk, dv, s_new


def setup_inputs(seed: int = 0) -> dict:
    key = jax.random.key(seed)
    ks = jax.random.split(key, 32)
    n_pages = PAST_LEN // PAGE_SIZE
    n_used = DEC_BATCH * n_pages
    n_pool = n_used + max(1, n_used // 4)

    def nrm(k, shape, scale):
        return jax.random.normal(k, shape, jnp.float32) * scale

    return {
        'x_prompt': nrm(ks[0], (BATCH, SEQ, D_MODEL), 1.0),
        'x_sample': nrm(ks[1], (DEC_BATCH, DEC_SEQ, D_MODEL), 1.0),
        'cache_k': nrm(ks[2], (DEPTH, n_pool, PAGE_SIZE, DIFF_HEADS, 2 * DIFF_HEAD_DIM), 1.0),
        'cache_v': nrm(ks[3], (DEPTH, n_pool, PAGE_SIZE, DIFF_HEADS, DIFF_V_DIM), 1.0),
        'state_gla': nrm(ks[4], (DEPTH, DEC_BATCH, GLA_HEADS, GLA_DK, GLA_DV), 1.0),
        'page_table': jax.random.permutation(ks[5], n_pool)[:n_used].reshape(DEC_BATCH, n_pages).astype(jnp.int32),
        'norm1_g': 1.0 + nrm(ks[6], (DEPTH, D_MODEL), 0.05),
        'w_in': nrm(ks[7], (DEPTH, D_MODEL, D_IN), D_MODEL ** -0.5),
        'w_a2': nrm(ks[8], (DEPTH, GLA_RANK, GLA_QK), GLA_RANK ** -0.5),
        'b_a': nrm(ks[9], (DEPTH, GLA_QK), 0.1),
        'gla_norm_g': 1.0 + nrm(ks[10], (DEPTH, GLA_DV), 0.05),
        'lambda_q1': nrm(ks[11], (DEPTH, DIFF_HEAD_DIM), 0.1),
        'lambda_k1': nrm(ks[12], (DEPTH, DIFF_HEAD_DIM), 0.1),
        'lambda_q2': nrm(ks[13], (DEPTH, DIFF_HEAD_DIM), 0.1),
        'lambda_k2': nrm(ks[14], (DEPTH, DIFF_HEAD_DIM), 0.1),
        'diff_norm_g': 1.0 + nrm(ks[15], (DEPTH, DIFF_V_DIM), 0.05),
        'w_out': nrm(ks[16], (DEPTH, D_MIX, D_MODEL), D_MIX ** -0.5),
        'norm2_g': 1.0 + nrm(ks[17], (DEPTH, D_MODEL), 0.05),
        'w_router_g': nrm(ks[18], (DEPTH, D_MODEL, N_GROUPS), D_MODEL ** -0.5),
        'b_router_g': nrm(ks[19], (DEPTH, N_GROUPS), 0.01),
        'w_router_e': nrm(ks[20], (DEPTH, N_GROUPS, D_MODEL, EXPERTS_PER_GROUP), D_MODEL ** -0.5),
        'b_router_e': nrm(ks[21], (DEPTH, N_GROUPS, EXPERTS_PER_GROUP), 0.01),
        'w_gate': nrm(ks[22], (DEPTH, N_EXPERTS, D_MODEL, EXPERT_FF), D_MODEL ** -0.5),
        'w_up': nrm(ks[23], (DEPTH, N_EXPERTS, D_MODEL, EXPERT_FF), D_MODEL ** -0.5),
        'w_down': nrm(ks[24], (DEPTH, N_EXPERTS, EXPERT_FF, D_MODEL), EXPERT_FF ** -0.5),
        'norm_f_g': 1.0 + nrm(ks[25], (D_MODEL,), 0.05),
    }


def reference(x_prompt, x_sample, cache_k, cache_v, state_gla, page_table,
              norm1_g, w_in, w_a2, b_a, gla_norm_g, lambda_q1, lambda_k1, lambda_q2, lambda_k2,
              diff_norm_g, w_out, norm2_g, w_router_g, b_router_g, w_router_e, b_router_e,
              w_gate, w_up, w_down, norm_f_g):
    dec_b, n_pages = page_table.shape
    page = cache_k.shape[2]
    past_len = n_pages * page
    xp, xs = x_prompt, x_sample
    kp_l, vp_l, sp_l, ks_l, vs_l, ss_l = [], [], [], [], [], []
    for l in range(DEPTH):
        lam_init = 0.8 - 0.6 * math.exp(-0.3 * l)
        lam = (jnp.exp(jnp.sum((lambda_q1[l] * lambda_k1[l]).astype(jnp.float32)))
               - jnp.exp(jnp.sum((lambda_q2[l] * lambda_k2[l]).astype(jnp.float32))) + lam_init)
        p = (norm1_g[l], w_in[l], w_a2[l], b_a[l], gla_norm_g[l], diff_norm_g[l], w_out[l], norm2_g[l],
             w_router_g[l], b_router_g[l], w_router_e[l], b_router_e[l], w_gate[l], w_up[l], w_down[l])
        s0 = jnp.zeros((xp.shape[0], GLA_HEADS, GLA_DK, GLA_DV), xp.dtype)
        xp, kp, vp, sp = hybrid_layer(xp, p, lam, lam_init, s0, None, None)
        past_k = cache_k[l, page_table].reshape(dec_b, past_len, DIFF_HEADS, 2 * DIFF_HEAD_DIM)
        past_v = cache_v[l, page_table].reshape(dec_b, past_len, DIFF_HEADS, DIFF_V_DIM)
        xs, ks_, vs_, ss = hybrid_layer(xs, p, lam, lam_init, state_gla[l], past_k, past_v)
        kp_l.append(kp); vp_l.append(vp); sp_l.append(sp)
        ks_l.append(ks_); vs_l.append(vs_); ss_l.append(ss)
    y_prompt = rmsnorm(xp, norm_f_g)
    y_sample = rmsnorm(xs, norm_f_g)
    k_prompt = jnp.stack(kp_l, axis=0)
    v_prompt = jnp.stack(vp_l, axis=0)
    gla_prompt = jnp.stack(sp_l, axis=0)
    k_sample = jnp.stack(ks_l, axis=0)
    v_sample = jnp.stack(vs_l, axis=0)
    gla_sample = jnp.stack(ss_l, axis=0)
    return (y_prompt, y_sample, k_prompt, v_prompt, gla_prompt, k_sample, v_sample, gla_sample)
```

```python
import functools
import math

import jax
import jax.numpy as jnp
from jax import lax
from jax.experimental import pallas as pl
from jax.experimental.pallas import tpu as pltpu

F32 = jnp.float32
BF16 = jnp.bfloat16

D_MODEL = 1024
GLA_HEADS = 4
GLA_DK = 64
GLA_DV = 128
GLA_RANK = 16
GLA_TAU = 16.0
GLA_CHUNK = 16
DIFF_HEADS = 4
DIFF_HEAD_DIM = 64
DIFF_V_DIM = 128
GLA_QK = GLA_HEADS * GLA_DK
GLA_V = GLA_HEADS * GLA_DV
DIFF_QK = DIFF_HEADS * 2 * DIFF_HEAD_DIM
DIFF_V = DIFF_HEADS * DIFF_V_DIM
N_GROUPS = 4
EXPERTS_PER_GROUP = 4
N_EXPERTS = 16
EXPERT_FF = 256
EPS = 1e-6
LANES = 128
VMEM_LIMIT = 56 * 1024 * 1024

_C_GQ, _C_GK, _C_GV, _C_GR, _C_DQ, _C_DK, _C_DV, _C_END = 0, 256, 512, 1024, 1536, 2048, 2560, 3072


def _rms(x, g):
    return x * lax.rsqrt(jnp.mean(x * x, axis=-1, keepdims=True) + EPS) * g


def _silu(x):
    return x * (1.0 / (1.0 + jnp.exp(-x)))


def _log_sigmoid(x):
    return jnp.minimum(x, 0.0) - jnp.log1p(jnp.exp(-jnp.abs(x)))


def _dot(a, b):
    return jnp.dot(a, b, preferred_element_type=F32)


def _dot_nt(a, b):
    return lax.dot_general(a, b, (((1,), (1,)), ((), ())), preferred_element_type=F32)


def _split3(x):
    hi = x.astype(BF16)
    r = x - hi.astype(F32)
    mid = r.astype(BF16)
    lo = (r - mid.astype(F32)).astype(BF16)
    return hi, mid, lo


def _params(sem):
    return pltpu.CompilerParams(dimension_semantics=sem, vmem_limit_bytes=VMEM_LIMIT)


def _inproj_kernel(x_ref, g_ref, wm_ref, wc_ref, wa2_ref, ba_ref,
                   gq_ref, gk_ref, la_ref, gv_ref, gr_ref, dqb_ref, dk_ref, dv_ref, dkb_ref, dvb_ref):
    h = _rms(x_ref[...], g_ref[...]).astype(BF16)

    def seg(lo, hi):
        return _dot(h, wm_ref[:, lo:hi])

    gq_ref[...] = seg(_C_GQ, _C_GK) * (GLA_DK ** -0.5)
    gk_ref[...] = seg(_C_GK, _C_GV)
    gv_ref[...] = seg(_C_GV, _C_GR)
    gr_ref[...] = seg(_C_GR, _C_DQ)
    dqb_ref[...] = (seg(_C_DQ, _C_DK) * (DIFF_HEAD_DIM ** -0.5)).astype(BF16)
    dk = seg(_C_DK, _C_DV)
    dk_ref[...] = dk
    dkb_ref[...] = dk.astype(BF16)
    dv = seg(_C_DV, _C_END)
    dv_ref[...] = dv
    dvb_ref[...] = dv.astype(BF16)
    code = _dot(h, wc_ref[...])
    pre = _dot(code.astype(BF16), wa2_ref[...]) + ba_ref[...]
    la_ref[...] = _log_sigmoid(pre) * (1.0 / GLA_TAU)


def _inproj(x, g1, wm, wc, wa2p, ba, tm):
    t = x.shape[0]
    row = lambda w: pl.BlockSpec((tm, w), lambda i: (i, 0))
    full = lambda a: pl.BlockSpec(a.shape, lambda i: (0,) * a.ndim)
    outs = [(GLA_QK, F32), (GLA_QK, F32), (GLA_QK, F32), (GLA_V, F32), (GLA_V, F32),
            (DIFF_QK, BF16), (DIFF_QK, F32), (DIFF_V, F32), (DIFF_QK, BF16), (DIFF_V, BF16)]
    return pl.pallas_call(
        _inproj_kernel,
        grid=(t // tm,),
        in_specs=[row(D_MODEL), full(g1), full(wm), full(wc), full(wa2p), full(ba)],
        out_specs=[row(w) for w, _ in outs],
        out_shape=[jax.ShapeDtypeStruct((t, w), dt) for w, dt in outs],
        compiler_params=_params(("parallel",)),
        name="inproj",
    )(x, g1, wm, wc, wa2p, ba)


def _gla_kernel(gq_ref, gk_ref, la_ref, gv_ref, gr_ref, tri_ref, last_ref, ind_ref, bdm_ref, g_ref, s0_ref,
                o_ref, sfin_ref,
                s_s, b_s, qi_s, kdt_s, blt_s, vb_s, o_s, p_s):
    l = pl.program_id(1)
    tl = gq_ref.shape[0]
    c16 = GLA_CHUNK

    @pl.when(l == 0)
    def _():
        s_s[...] = jnp.zeros_like(s_s)
        for h in range(GLA_HEADS):
            s_s[h * GLA_DK:(h + 1) * GLA_DK, h * GLA_DV:(h + 1) * GLA_DV] = s0_ref[0, h]

    hi, mid, lo = _split3(la_ref[...])

    def cs(m):
        return (_dot(m, lo) + _dot(m, mid)) + _dot(m, hi)

    b = cs(tri_ref[...])
    bl = cs(last_ref[...])
    b_s[...] = b
    qi_s[...] = gq_ref[...] * jnp.exp(b)
    kdt_s[...] = (gk_ref[...] * jnp.exp(bl - b)).T
    blt_s[...] = bl.T
    vb_s[...] = gv_ref[...].astype(BF16)

    rowi = lax.broadcasted_iota(jnp.int32, (c16, GLA_QK), 0)
    lane_t = lax.broadcasted_iota(jnp.int32, (GLA_QK, tl), 1)

    def chunk(c, carry):
        r0 = pl.multiple_of(c * c16, c16)
        q_c = gq_ref[pl.ds(r0, c16), :]
        b_c = b_s[pl.ds(r0, c16), :]
        for j in range(c16):
            bj = b_s[pl.ds(r0 + j, 1), :]
            kj = gk_ref[pl.ds(r0 + j, 1), :]
            valid = rowi >= j
            e = jnp.exp(jnp.where(valid, b_c - bj, 0.0))
            p_s[j * c16:(j + 1) * c16, :] = jnp.where(valid, q_c * e * kj, 0.0).astype(BF16)
        r = _dot(p_s[...], ind_ref[...])
        o = _dot(qi_s[pl.ds(r0, c16), :].astype(BF16), s_s[...].astype(BF16))
        for j in range(c16):
            o = o + r[j * c16:(j + 1) * c16, :] * gv_ref[pl.ds(r0 + j, 1), :]
        o_s[pl.ds(r0, c16), :] = o
        kdm = jnp.where(lane_t // c16 == c, kdt_s[...], 0.0).astype(BF16)
        u = _dot(kdm, vb_s[...])
        a_col = jnp.exp(jnp.sum(jnp.where(lane_t == r0, blt_s[...], 0.0), axis=1, keepdims=True))
        s_s[...] = a_col * s_s[...] + bdm_ref[...] * u
        return carry

    lax.fori_loop(0, tl // c16, chunk, 0)

    o = o_s[...]
    for h in range(GLA_HEADS):
        sl = slice(h * GLA_DV, (h + 1) * GLA_DV)
        o_ref[:, sl] = (_rms(o[:, sl], g_ref[...]) * _silu(gr_ref[:, sl])).astype(o_ref.dtype)

    @pl.when(l == pl.num_programs(1) - 1)
    def _():
        for h in range(GLA_HEADS):
            sfin_ref[0, h] = s_s[h * GLA_DK:(h + 1) * GLA_DK, h * GLA_DV:(h + 1) * GLA_DV]


def _gla_prompt(gq, gk, la, gv, gr, gla_g, s0, nb, tl):
    t = gq.shape[0]
    nl = t // nb // tl
    ti = jnp.arange(tl)
    same = (ti[:, None] // GLA_CHUNK) == (ti[None, :] // GLA_CHUNK)
    tri = (same & (ti[None, :] <= ti[:, None])).astype(BF16)
    last = same.astype(BF16)
    hq = jnp.arange(GLA_QK) // GLA_DK
    hv = jnp.arange(GLA_V) // GLA_DV
    bd = hq[:, None] == hv[None, :]
    row = lambda w: pl.BlockSpec((tl, w), lambda b, l: (b * nl + l, 0))
    full = lambda shp: pl.BlockSpec(shp, lambda b, l: (0,) * len(shp))
    st = pl.BlockSpec((1, GLA_HEADS, GLA_DK, GLA_DV), lambda b, l: (b, 0, 0, 0))
    return pl.pallas_call(
        _gla_kernel,
        grid=(nb, nl),
        in_specs=[row(GLA_QK), row(GLA_QK), row(GLA_QK), row(GLA_V), row(GLA_V),
                  full((tl, tl)), full((tl, tl)), full((GLA_QK, GLA_V)), full((GLA_QK, GLA_V)),
                  full((1, GLA_DV)), st],
        out_specs=[row(GLA_V), st],
        out_shape=[jax.ShapeDtypeStruct((t, GLA_V), BF16),
                   jax.ShapeDtypeStruct((nb, GLA_HEADS, GLA_DK, GLA_DV), F32)],
        scratch_shapes=[pltpu.VMEM((GLA_QK, GLA_V), F32), pltpu.VMEM((tl, GLA_QK), F32),
                        pltpu.VMEM((tl, GLA_QK), F32), pltpu.VMEM((GLA_QK, tl), F32),
                        pltpu.VMEM((GLA_QK, tl), F32), pltpu.VMEM((tl, GLA_V), BF16),
                        pltpu.VMEM((tl, GLA_V), F32), pltpu.VMEM((GLA_CHUNK * GLA_CHUNK, GLA_QK), BF16)],
        compiler_params=_params(("parallel", "arbitrary")),
        name="gla_prompt",
    )(gq, gk, la, gv, gr, tri, last, bd.astype(BF16), bd.astype(F32), gla_g, s0)


def _gla_step_kernel(q_ref, k_ref, la_ref, v_ref, gr_ref, g_ref, s_ref, o_ref, snew_ref):
    for h in range(GLA_HEADS):
        dk = slice(h * GLA_DK, (h + 1) * GLA_DK)
        dv = slice(h * GLA_DV, (h + 1) * GLA_DV)
        a = jnp.exp(la_ref[0, dk, :])
        s_new = a * s_ref[0, h] + k_ref[0, dk, :] * v_ref[0, :, dv]
        snew_ref[0, h] = s_new
        o = jnp.sum(q_ref[0, dk, :] * s_new, axis=0, keepdims=True)
        o_ref[0, :, dv] = _rms(o, g_ref[...]) * _silu(gr_ref[0, :, dv])


def _gla_step(gq, gk, la, gv, gr, gla_g, state):
    n = gq.shape[0]
    col = pl.BlockSpec((1, GLA_QK, 1), lambda b: (b, 0, 0))
    rowv = pl.BlockSpec((1, 1, GLA_V), lambda b: (b, 0, 0))
    st = pl.BlockSpec((1, GLA_HEADS, GLA_DK, GLA_DV), lambda b: (b, 0, 0, 0))
    return pl.pallas_call(
        _gla_step_kernel,
        grid=(n,),
        in_specs=[col, col, col, rowv, rowv, pl.BlockSpec((1, GLA_DV), lambda b: (0, 0)), st],
        out_specs=[rowv, st],
        out_shape=[jax.ShapeDtypeStruct((n, 1, GLA_V), F32),
                   jax.ShapeDtypeStruct((n, GLA_HEADS, GLA_DK, GLA_DV), F32)],
        compiler_params=_params(("parallel",)),
        name="gla_step",
    )(gq.reshape(n, GLA_QK, 1), gk.reshape(n, GLA_QK, 1), la.reshape(n, GLA_QK, 1),
      gv.reshape(n, 1, GLA_V), gr.reshape(n, 1, GLA_V), gla_g, state)


def _lambda(lq1_ref, lk1_ref, lq2_ref, lk2_ref, lam_init):
    e1 = jnp.exp(jnp.sum(lq1_ref[...] * lk1_ref[...], axis=-1, keepdims=True))
    e2 = jnp.exp(jnp.sum(lq2_ref[...] * lk2_ref[...], axis=-1, keepdims=True))
    return e1 - e2 + lam_init


def _softmax_step(s, v, m, l, acc):
    m_new = jnp.maximum(m, jnp.max(s, axis=-1, keepdims=True))
    alpha = jnp.exp(m - m_new)
    p = jnp.exp(s - m_new)
    l = alpha * l + jnp.sum(p, axis=-1, keepdims=True)
    acc = alpha * acc + _dot(p.astype(BF16), v)
    return m_new, l, acc


def _attn_kernel(lq1_ref, lk1_ref, lq2_ref, lk2_ref, q_ref, k_ref, v_ref, g_ref, o_ref, *, lam_init):
    i = pl.program_id(2)
    tq = q_ref.shape[0]
    q = q_ref[...]
    lane = lax.broadcasted_iota(jnp.int32, q.shape, 1)
    zero = jnp.zeros_like(q)
    q1 = jnp.where(lane < DIFF_HEAD_DIM, q, zero)
    q2 = jnp.where(lane >= DIFF_HEAD_DIM, q, zero)
    row = lax.broadcasted_iota(jnp.int32, (tq, tq), 0)
    colk = lax.broadcasted_iota(jnp.int32, (tq, tq), 1)

    def tile(j, carry, masked):
        m1, l1, a1, m2, l2, a2 = carry
        r0 = pl.multiple_of(j * tq, tq)
        k = k_ref[pl.ds(r0, tq), :]
        v = v_ref[pl.ds(r0, tq), :]
        s1 = _dot_nt(q1, k)
        s2 = _dot_nt(q2, k)
        if masked:
            keep = colk <= row
            s1 = jnp.where(keep, s1, -jnp.inf)
            s2 = jnp.where(keep, s2, -jnp.inf)
        m1, l1, a1 = _softmax_step(s1, v, m1, l1, a1)
        m2, l2, a2 = _softmax_step(s2, v, m2, l2, a2)
        return m1, l1, a1, m2, l2, a2

    minf = jnp.full((tq, 1), -jnp.inf, F32)
    z1 = jnp.zeros((tq, 1), F32)
    za = jnp.zeros((tq, DIFF_V_DIM), F32)
    carry = lax.fori_loop(0, i, functools.partial(tile, masked=False), (minf, z1, za, minf, z1, za))
    m1, l1, a1, m2, l2, a2 = tile(i, carry, True)
    lam = _lambda(lq1_ref, lk1_ref, lq2_ref, lk2_ref, lam_init)
    out = a1 / l1 - lam * (a2 / l2)
    o_ref[...] = (_rms(out, g_ref[...]) * (1.0 - lam_init)).astype(o_ref.dtype)


def _attn_prompt(lams, dqb, dkb, dvb, diff_g, nb, seq, tq, lam_init):
    t = dqb.shape[0]
    nq = seq // tq
    lam_spec = pl.BlockSpec((1, DIFF_HEAD_DIM), lambda b, h, i: (0, 0))
    qspec = pl.BlockSpec((tq, LANES), lambda b, h, i: (b * nq + i, h))
    kvspec = pl.BlockSpec((seq, LANES), lambda b, h, i: (b, h))
    return pl.pallas_call(
        functools.partial(_attn_kernel, lam_init=lam_init),
        grid=(nb, DIFF_HEADS, nq),
        in_specs=[lam_spec] * 4 + [qspec, kvspec, kvspec, pl.BlockSpec((1, DIFF_V_DIM), lambda b, h, i: (0, 0))],
        out_specs=qspec,
        out_shape=jax.ShapeDtypeStruct((t, DIFF_V), BF16),
        compiler_params=_params(("parallel", "parallel", "arbitrary")),
        name="attn_prompt",
    )(*lams, dqb, dkb, dvb, diff_g)


def _decode_kernel(pt_ref, lq1_ref, lk1_ref, lq2_ref, lk2_ref, q_ref, kn_ref, vn_ref, g_ref, ck_ref, cv_ref,
                   o_ref, kbuf, vbuf, sem, m_s, l_s, acc_s, *, lam_init, ppc, nch, nseq):
    b = pl.program_id(0)
    c = pl.program_id(1)
    step = b * nch + c
    slot = lax.rem(step, 2)

    def copies(bb, cc, sl):
        out = []
        for p in range(ppc):
            page = pt_ref[bb, cc * ppc + p]
            out.append(pltpu.make_async_copy(ck_ref.at[0, page], kbuf.at[sl, p], sem.at[0, sl]))
            out.append(pltpu.make_async_copy(cv_ref.at[0, page], vbuf.at[sl, p], sem.at[1, sl]))
        return out

    @pl.when(step == 0)
    def _():
        for cp in copies(b, c, slot):
            cp.start()

    @pl.when(step + 1 < nseq * nch)
    def _():
        nxt = step + 1
        for cp in copies(nxt // nch, lax.rem(nxt, nch), 1 - slot):
            cp.start()

    @pl.when(c == 0)
    def _():
        m_s[...] = jnp.full_like(m_s, -jnp.inf)
        l_s[...] = jnp.zeros_like(l_s)
        acc_s[...] = jnp.zeros_like(acc_s)

    for cp in copies(b, c, slot):
        cp.wait()

    rowi = lax.broadcasted_iota(jnp.int32, (8, LANES), 0)
    lane = lax.broadcasted_iota(jnp.int32, (8, LANES), 1)
    qsel = ((rowi == 0) & (lane < DIFF_HEAD_DIM)) | ((rowi == 1) & (lane >= DIFF_HEAD_DIM))

    def q_rows(h):
        qh = jnp.broadcast_to(q_ref[0, :, h * LANES:(h + 1) * LANES].astype(F32), (8, LANES))
        return jnp.where(qsel, qh, 0.0)

    for h in range(DIFF_HEADS):
        kh = kbuf[slot, :, :, h, :].reshape(ppc * 128, LANES).astype(BF16)
        vh = vbuf[slot, :, :, h, :].reshape(ppc * 128, LANES).astype(BF16)
        s = _dot_nt(q_rows(h).astype(BF16), kh)
        m, l, acc = _softmax_step(s, vh, m_s[h], l_s[h], acc_s[h])
        m_s[h] = m
        l_s[h] = l
        acc_s[h] = acc

    @pl.when(c == nch - 1)
    def _():
        lam = _lambda(lq1_ref, lk1_ref, lq2_ref, lk2_ref, lam_init)
        for h in range(DIFF_HEADS):
            sl = slice(h * LANES, (h + 1) * LANES)
            kn = kn_ref[0, :, sl].astype(BF16)
            vn = vn_ref[0, :, sl].astype(BF16).astype(F32)
            s = jnp.sum(q_rows(h) * kn.astype(F32), axis=-1, keepdims=True)
            m = m_s[h]
            m_new = jnp.maximum(m, s)
            alpha = jnp.exp(m - m_new)
            p = jnp.exp(s - m_new)
            l = alpha * l_s[h] + p
            acc = alpha * acc_s[h] + p.astype(BF16).astype(F32) * vn
            w = acc / l
            out = w[0:1, :] - lam * w[1:2, :]
            o_ref[0, :, sl] = _rms(out, g_ref[...]) * (1.0 - lam_init)


def _attn_decode(page_table, lams, dqb, dk, dv, diff_g, cache_k, cache_v, lam_init, ppc):
    n, n_pages = page_table.shape
    nch = n_pages // ppc
    page = cache_k.shape[2]
    lam_spec = pl.BlockSpec((1, DIFF_HEAD_DIM), lambda b, c, pt: (0, 0))
    rowspec = pl.BlockSpec((1, 1, DIFF_QK), lambda b, c, pt: (b, 0, 0))
    anyspec = pl.BlockSpec(memory_space=pl.ANY)
    kern = functools.partial(_decode_kernel, lam_init=lam_init, ppc=ppc, nch=nch, nseq=n)
    return pl.pallas_call(
        kern,
        grid_spec=pltpu.PrefetchScalarGridSpec(
            num_scalar_prefetch=1,
            grid=(n, nch),
            in_specs=[lam_spec] * 4 + [rowspec, rowspec, rowspec,
                                      pl.BlockSpec((1, DIFF_V_DIM), lambda b, c, pt: (0, 0)), anyspec, anyspec],
            out_specs=rowspec,
            scratch_shapes=[pltpu.VMEM((2, ppc, page, DIFF_HEADS, 2 * DIFF_HEAD_DIM), F32),
                            pltpu.VMEM((2, ppc, page, DIFF_HEADS, DIFF_V_DIM), F32),
                            pltpu.SemaphoreType.DMA((2, 2)),
                            pltpu.VMEM((DIFF_HEADS, 8, 1), F32), pltpu.VMEM((DIFF_HEADS, 8, 1), F32),
                            pltpu.VMEM((DIFF_HEADS, 8, DIFF_V_DIM), F32)]),
        out_shape=jax.ShapeDtypeStruct((n, 1, DIFF_V), F32),
        compiler_params=_params(("arbitrary", "arbitrary")),
        name="attn_decode",
    )(page_table, *lams, dqb.reshape(n, 1, DIFF_QK), dk.reshape(n, 1, DIFF_QK), dv.reshape(n, 1, DIFF_V),
      diff_g, cache_k, cache_v)


def _post_kernel(x_ref, go_ref, do_ref, wo_ref, g2_ref, wr_ref, br_ref, x1_ref, t_ref, gates_ref):
    x1 = x_ref[...] + _dot(go_ref[...].astype(BF16), wo_ref[:GLA_V, :]) + _dot(do_ref[...].astype(BF16), wo_ref[GLA_V:, :])
    x1_ref[...] = x1
    t = _rms(x1, g2_ref[...])
    t_ref[...] = t.astype(BF16)
    wr = wr_ref[...]
    t_hi = t.astype(BF16)
    t_lo = (t - t_hi.astype(F32)).astype(BF16)
    w_hi = wr.astype(BF16)
    w_lo = (wr - w_hi.astype(F32)).astype(BF16)
    logits = (_dot(t_lo, w_hi) + _dot(t_hi, w_lo)) + _dot(t_hi, w_hi) + br_ref[...]
    lg = logits[:, :N_GROUPS]
    le_all = logits[:, N_GROUPS:N_GROUPS + N_EXPERTS]
    gi = lax.broadcasted_iota(jnp.int32, lg.shape, 1)
    lg_max = jnp.max(lg, axis=-1, keepdims=True)
    g = jnp.min(jnp.where(lg == lg_max, gi, N_GROUPS), axis=-1, keepdims=True)
    p_top = 1.0 / jnp.sum(jnp.exp(lg - lg_max), axis=-1, keepdims=True)
    ei = lax.broadcasted_iota(jnp.int32, le_all.shape, 1)
    in_g = (ei // EXPERTS_PER_GROUP) == g
    neg = -jnp.inf
    le = jnp.where(in_g, le_all, neg)
    v1 = jnp.max(le, axis=-1, keepdims=True)
    i1 = jnp.min(jnp.where(le == v1, ei, N_EXPERTS), axis=-1, keepdims=True)
    le2 = jnp.where(ei == i1, neg, le)
    v2 = jnp.max(le2, axis=-1, keepdims=True)
    i2 = jnp.min(jnp.where(le2 == v2, ei, N_EXPERTS), axis=-1, keepdims=True)
    e2 = jnp.exp(v2 - v1)
    w1 = p_top / (1.0 + e2)
    w2 = p_top * e2 / (1.0 + e2)
    gates_ref[...] = jnp.where(ei == i1, w1, 0.0) + jnp.where(ei == i2, w2, 0.0)


def _post(x, gla_o, diff_o, wo, g2, wr, br, tm):
    t = x.shape[0]
    row = lambda w: pl.BlockSpec((tm, w), lambda i: (i, 0))
    full = lambda a: pl.BlockSpec(a.shape, lambda i: (0,) * a.ndim)
    return pl.pallas_call(
        _post_kernel,
        grid=(t // tm,),
        in_specs=[row(D_MODEL), row(GLA_V), row(DIFF_V), full(wo), full(g2), full(wr), full(br)],
        out_specs=[row(D_MODEL), row(D_MODEL), row(N_EXPERTS)],
        out_shape=[jax.ShapeDtypeStruct((t, D_MODEL), F32), jax.ShapeDtypeStruct((t, D_MODEL), BF16),
                   jax.ShapeDtypeStruct((t, N_EXPERTS), F32)],
        compiler_params=_params(("parallel",)),
        name="post",
    )(x, gla_o, diff_o, wo, g2, wr, br)


def _moe_kernel(t_ref, gates_ref, x1_ref, wgu_ref, wd_ref, gf_ref, y_ref, acc_s, *, eps_per_step):
    e0 = pl.program_id(1)

    @pl.when(e0 == 0)
    def _():
        acc_s[...] = jnp.zeros_like(acc_s)

    t = t_ref[...]
    gates = gates_ref[...]
    ei = lax.broadcasted_iota(jnp.int32, gates.shape, 1)
    acc = acc_s[...]
    for k in range(eps_per_step):
        gu = _dot(t, wgu_ref[k])
        gate = jnp.sum(jnp.where(ei == e0 * eps_per_step + k, gates, 0.0), axis=-1, keepdims=True)
        hid = _silu(gu[:, :EXPERT_FF]) * gu[:, EXPERT_FF:] * gate
        acc = acc + _dot(hid.astype(BF16), wd_ref[k])
    acc_s[...] = acc

    @pl.when(e0 == pl.num_programs(1) - 1)
    def _():
        y_ref[...] = _rms(x1_ref[...] + acc_s[...], gf_ref[...])


def _moe(t, gates, x1, wgu, wd, gf, tm, eps_per_step):
    n = t.shape[0]
    row = lambda w: pl.BlockSpec((tm, w), lambda i, e: (i, 0))
    return pl.pallas_call(
        functools.partial(_moe_kernel, eps_per_step=eps_per_step),
        grid=(n // tm, N_EXPERTS // eps_per_step),
        in_specs=[row(D_MODEL), row(N_EXPERTS), row(D_MODEL),
                  pl.BlockSpec((eps_per_step, D_MODEL, 2 * EXPERT_FF), lambda i, e: (e, 0, 0)),
                  pl.BlockSpec((eps_per_step, EXPERT_FF, D_MODEL), lambda i, e: (e, 0, 0)),
                  pl.BlockSpec((1, D_MODEL), lambda i, e: (0, 0))],
        out_specs=row(D_MODEL),
        out_shape=jax.ShapeDtypeStruct((n, D_MODEL), F32),
        scratch_shapes=[pltpu.VMEM((tm, D_MODEL), F32)],
        compiler_params=_params(("parallel", "arbitrary")),
        name="moe",
    )(t, gates, x1, wgu, wd, gf)


def kernel(x_prompt, x_sample, cache_k, cache_v, state_gla, page_table, norm1_g, w_in, w_a2, b_a, gla_norm_g,
           lambda_q1, lambda_k1, lambda_q2, lambda_k2, diff_norm_g, w_out, norm2_g, w_router_g, b_router_g,
           w_router_e, b_router_e, w_gate, w_up, w_down, norm_f_g):
    nb, seq, _ = x_prompt.shape
    ns = x_sample.shape[0]
    depth = w_in.shape[0]
    assert depth == 1 and x_sample.shape[1] == 1
    lam_init = 0.8 - 0.6 * math.exp(-0.3 * 0)

    c_code = 2 * GLA_QK + 2 * GLA_V
    w = w_in[0]
    wm = jnp.concatenate([w[:, :c_code], w[:, c_code + GLA_RANK:]], axis=1).astype(BF16)
    wc = jnp.pad(w[:, c_code:c_code + GLA_RANK], ((0, 0), (0, LANES - GLA_RANK))).astype(BF16)
    wa2p = jnp.pad(w_a2[0], ((0, LANES - GLA_RANK), (0, 0))).astype(BF16)
    ba = b_a[0].reshape(1, GLA_QK)
    g1 = norm1_g[0].reshape(1, D_MODEL)
    g2 = norm2_g[0].reshape(1, D_MODEL)
    gf = norm_f_g.reshape(1, D_MODEL)
    gla_g = gla_norm_g[0].reshape(1, GLA_DV)
    diff_g = diff_norm_g[0].reshape(1, DIFF_V_DIM)
    lams = [a[0].reshape(1, DIFF_HEAD_DIM) for a in (lambda_q1, lambda_k1, lambda_q2, lambda_k2)]
    wo = w_out[0].astype(BF16)
    wr = jnp.concatenate([w_router_g[0], w_router_e[0].transpose(1, 0, 2).reshape(D_MODEL, N_EXPERTS)], axis=1)
    wr = jnp.pad(wr, ((0, 0), (0, LANES - N_GROUPS - N_EXPERTS)))
    br = jnp.pad(jnp.concatenate([b_router_g[0], b_router_e[0].reshape(N_EXPERTS)]),
                 (0, LANES - N_GROUPS - N_EXPERTS)).reshape(1, LANES)
    wgu = jnp.concatenate([w_gate[0], w_up[0]], axis=-1).astype(BF16)
    wd = w_down[0].astype(BF16)

    def dense_tail(x, gla_o, diff_o, tm_post, tm_moe):
        x1, t, gates = _post(x, gla_o, diff_o, wo, g2, wr, br, tm_post)
        return _moe(t, gates, x1, wgu, wd, gf, tm_moe, EXPERTS_PER_GROUP)

    xp = x_prompt.reshape(nb * seq, D_MODEL)
    gq, gk, la, gv, gr, dqb, dk, dv, dkb, dvb = _inproj(xp, g1, wm, wc, wa2p, ba, 256)
    s0 = jnp.zeros((nb, GLA_HEADS, GLA_DK, GLA_DV), F32)
    gla_o, s_prompt = _gla_prompt(gq, gk, la, gv, gr, gla_g, s0, nb, 256)
    diff_o = _attn_prompt(lams, dqb, dkb, dvb, diff_g, nb, seq, 512, lam_init)
    y_prompt = dense_tail(xp, gla_o, diff_o, 256, 1024).reshape(nb, seq, D_MODEL)

    xs = x_sample.reshape(ns, D_MODEL)
    sgq, sgk, sla, sgv, sgr, sdqb, sdk, sdv, _, _ = _inproj(xs, g1, wm, wc, wa2p, ba, ns)
    sgla_o, s_sample = _gla_step(sgq, sgk, sla, sgv, sgr, gla_g, state_gla[0])
    sdiff_o = _attn_decode(page_table, lams, sdqb, sdk, sdv, diff_g, cache_k, cache_v, lam_init, 8)
    y_sample = dense_tail(xs, sgla_o.reshape(ns, GLA_V), sdiff_o.reshape(ns, DIFF_V), ns, ns).reshape(ns, 1, D_MODEL)

    k_prompt = dk.reshape(1, nb, seq, DIFF_HEADS, 2 * DIFF_HEAD_DIM)
    v_prompt = dv.reshape(1, nb, seq, DIFF_HEADS, DIFF_V_DIM)
    k_sample = sdk.reshape(1, ns, 1, DIFF_HEADS, 2 * DIFF_HEAD_DIM)
    v_sample = sdv.reshape(1, ns, 1, DIFF_HEADS, DIFF_V_DIM)
    return (y_prompt, y_sample, k_prompt, v_prompt, s_prompt[None], k_sample, v_sample, s_sample[None])
```

```python
import functools
import math

import jax
import jax.numpy as jnp
from jax import lax
from jax.experimental import pallas as pl
from jax.experimental.pallas import tpu as pltpu

F32 = jnp.float32
BF16 = jnp.bfloat16

D_MODEL = 1024
GLA_HEADS = 4
GLA_DK = 64
GLA_DV = 128
GLA_RANK = 16
GLA_TAU = 16.0
GLA_CHUNK = 16
DIFF_HEADS = 4
DIFF_HEAD_DIM = 64
DIFF_V_DIM = 128
GLA_QK = GLA_HEADS * GLA_DK
GLA_V = GLA_HEADS * GLA_DV
DIFF_QK = DIFF_HEADS * 2 * DIFF_HEAD_DIM
DIFF_V = DIFF_HEADS * DIFF_V_DIM
N_GROUPS = 4
EXPERTS_PER_GROUP = 4
N_EXPERTS = 16
EXPERT_FF = 256
EPS = 1e-6
LANES = 128
BF16_SUBLANES = 16
LOG2E = math.log2(math.e)
VT_ROWS = DIFF_V_DIM + BF16_SUBLANES
VMEM_LIMIT = 56 * 1024 * 1024

_C_GQ, _C_GK, _C_GV, _C_GR, _C_DQ, _C_DK, _C_DV, _C_END = 0, 256, 512, 1024, 1536, 2048, 2560, 3072


def _rms(x, g):
    return x * lax.rsqrt(jnp.mean(x * x, axis=-1, keepdims=True) + EPS) * g


def _silu(x):
    return x * (1.0 / (1.0 + jnp.exp(-x)))


def _log_sigmoid(x):
    return jnp.minimum(x, 0.0) - jnp.log1p(jnp.exp(-jnp.abs(x)))


def _dot(a, b):
    return jnp.dot(a, b, preferred_element_type=F32)


def _dot_nt(a, b):
    return lax.dot_general(a, b, (((1,), (1,)), ((), ())), preferred_element_type=F32)


def _split3(x):
    hi = x.astype(BF16)
    r = x - hi.astype(F32)
    mid = r.astype(BF16)
    lo = (r - mid.astype(F32)).astype(BF16)
    return hi, mid, lo


def _params(sem):
    return pltpu.CompilerParams(dimension_semantics=sem, vmem_limit_bytes=VMEM_LIMIT)


def _inproj_kernel(x_ref, g_ref, wm_ref, wc_ref, wa2_ref, ba_ref,
                   gq_ref, gk_ref, la_ref, gv_ref, gr_ref, dqb_ref, dk_ref, dv_ref, *prompt_refs):
    h = _rms(x_ref[...], g_ref[...]).astype(BF16)

    def seg(lo, hi):
        return _dot(h, wm_ref[:, lo:hi])

    gq_ref[...] = seg(_C_GQ, _C_GK) * (GLA_DK ** -0.5)
    gk_ref[...] = seg(_C_GK, _C_GV)
    gv_ref[...] = seg(_C_GV, _C_GR)
    gr_ref[...] = seg(_C_GR, _C_DQ)
    dqb_ref[...] = (seg(_C_DQ, _C_DK) * (DIFF_HEAD_DIM ** -0.5 * LOG2E)).astype(BF16)
    dk = seg(_C_DK, _C_DV)
    dv = seg(_C_DV, _C_END)
    tm = dk.shape[0]
    for hd in range(DIFF_HEADS):
        rows = pl.ds(hd, tm, stride=DIFF_HEADS)
        dk_ref[rows, :] = dk[:, hd * LANES:(hd + 1) * LANES]
        dv_ref[rows, :] = dv[:, hd * LANES:(hd + 1) * LANES]
    if prompt_refs:
        dkb_ref, vt_ref = prompt_refs
        dkb_ref[...] = dk.astype(BF16)
        ones_row = (lax.broadcasted_iota(jnp.int32, (VT_ROWS - DIFF_V_DIM, tm), 0) == 0).astype(F32)
        for hd in range(DIFF_HEADS):
            vt_ref[0, hd, :DIFF_V_DIM, :] = dv[:, hd * DIFF_V_DIM:(hd + 1) * DIFF_V_DIM].T.astype(BF16)
            vt_ref[0, hd, DIFF_V_DIM:, :] = ones_row.astype(BF16)
    code = _dot(h, wc_ref[...])
    pre = _dot(code.astype(BF16), wa2_ref[...]) + ba_ref[...]
    la_ref[...] = _log_sigmoid(pre) * (1.0 / GLA_TAU)


def _inproj(x, g1, wm, wc, wa2p, ba, tm, prompt_shape=None):
    t = x.shape[0]
    row = lambda w: pl.BlockSpec((tm, w), lambda i: (i, 0))
    full = lambda a: pl.BlockSpec(a.shape, lambda i: (0,) * a.ndim)
    outs = [(GLA_QK, F32), (GLA_QK, F32), (GLA_QK, F32), (GLA_V, F32), (GLA_V, F32), (DIFF_QK, BF16)]
    out_specs = [row(w) for w, _ in outs]
    out_shape = [jax.ShapeDtypeStruct((t, w), dt) for w, dt in outs]
    out_specs += [pl.BlockSpec((tm * DIFF_HEADS, LANES), lambda i: (i, 0))] * 2
    out_shape += [jax.ShapeDtypeStruct((t * DIFF_HEADS, LANES), F32)] * 2
    if prompt_shape is not None:
        nb, seq = prompt_shape
        nl = seq // tm
        out_specs += [row(DIFF_QK), pl.BlockSpec((1, DIFF_HEADS, VT_ROWS, tm), lambda i: (i // nl, 0, 0, i % nl))]
        out_shape += [jax.ShapeDtypeStruct((t, DIFF_QK), BF16),
                      jax.ShapeDtypeStruct((nb, DIFF_HEADS, VT_ROWS, seq), BF16)]
    return pl.pallas_call(
        _inproj_kernel,
        grid=(t // tm,),
        in_specs=[row(D_MODEL), full(g1), full(wm), full(wc), full(wa2p), full(ba)],
        out_specs=out_specs,
        out_shape=out_shape,
        compiler_params=_params(("parallel",)),
        name="inproj",
    )(x, g1, wm, wc, wa2p, ba)


def _gla_kernel(gq_ref, gk_ref, la_ref, gv_ref, gr_ref, tri_ref, last_ref, ind_ref, bdm_ref, g_ref, s0_ref,
                o_ref, sfin_ref,
                s_s, b_s, qi_s, kdt_s, blt_s, vb_s, o_s, p_s):
    l = pl.program_id(1)
    tl = gq_ref.shape[0]
    c16 = GLA_CHUNK

    @pl.when(l == 0)
    def _():
        s_s[...] = jnp.zeros_like(s_s)
        for h in range(GLA_HEADS):
            s_s[h * GLA_DK:(h + 1) * GLA_DK, h * GLA_DV:(h + 1) * GLA_DV] = s0_ref[0, h]

    hi, mid, lo = _split3(la_ref[...])

    def cs(m):
        return (_dot(m, lo) + _dot(m, mid)) + _dot(m, hi)

    b = cs(tri_ref[...])
    bl = cs(last_ref[...])
    b_s[...] = b
    qi_s[...] = gq_ref[...] * jnp.exp(b)
    kdt_s[...] = (gk_ref[...] * jnp.exp(bl - b)).T
    blt_s[...] = bl.T
    vb_s[...] = gv_ref[...].astype(BF16)

    rowi = lax.broadcasted_iota(jnp.int32, (c16, GLA_QK), 0)
    lane_t = lax.broadcasted_iota(jnp.int32, (GLA_QK, tl), 1)

    def chunk(c, carry):
        r0 = pl.multiple_of(c * c16, c16)
        q_c = gq_ref[pl.ds(r0, c16), :]
        b_c = b_s[pl.ds(r0, c16), :]
        for j in range(c16):
            bj = b_s[pl.ds(r0 + j, 1), :]
            kj = gk_ref[pl.ds(r0 + j, 1), :]
            valid = rowi >= j
            e = jnp.exp(jnp.where(valid, b_c - bj, 0.0))
            p_s[j * c16:(j + 1) * c16, :] = jnp.where(valid, q_c * e * kj, 0.0).astype(BF16)
        r = _dot(p_s[...], ind_ref[...])
        o = _dot(qi_s[pl.ds(r0, c16), :].astype(BF16), s_s[...].astype(BF16))
        for j in range(c16):
            o = o + r[j * c16:(j + 1) * c16, :] * gv_ref[pl.ds(r0 + j, 1), :]
        o_s[pl.ds(r0, c16), :] = o
        kdm = jnp.where(lane_t // c16 == c, kdt_s[...], 0.0).astype(BF16)
        u = _dot(kdm, vb_s[...])
        a_col = jnp.exp(jnp.sum(jnp.where(lane_t == r0, blt_s[...], 0.0), axis=1, keepdims=True))
        s_s[...] = a_col * s_s[...] + bdm_ref[...] * u
        return carry

    lax.fori_loop(0, tl // c16, chunk, 0)

    o = o_s[...]
    for h in range(GLA_HEADS):
        sl = slice(h * GLA_DV, (h + 1) * GLA_DV)
        o_ref[:, sl] = (_rms(o[:, sl], g_ref[...]) * _silu(gr_ref[:, sl])).astype(o_ref.dtype)

    @pl.when(l == pl.num_programs(1) - 1)
    def _():
        for h in range(GLA_HEADS):
            sfin_ref[0, h] = s_s[h * GLA_DK:(h + 1) * GLA_DK, h * GLA_DV:(h + 1) * GLA_DV]


def _gla_prompt(gq, gk, la, gv, gr, gla_g, s0, nb, tl):
    t = gq.shape[0]
    nl = t // nb // tl
    ti = jnp.arange(tl)
    same = (ti[:, None] // GLA_CHUNK) == (ti[None, :] // GLA_CHUNK)
    tri = (same & (ti[None, :] <= ti[:, None])).astype(BF16)
    last = same.astype(BF16)
    hq = jnp.arange(GLA_QK) // GLA_DK
    hv = jnp.arange(GLA_V) // GLA_DV
    bd = hq[:, None] == hv[None, :]
    row = lambda w: pl.BlockSpec((tl, w), lambda b, l: (b * nl + l, 0))
    full = lambda shp: pl.BlockSpec(shp, lambda b, l: (0,) * len(shp))
    st = pl.BlockSpec((1, GLA_HEADS, GLA_DK, GLA_DV), lambda b, l: (b, 0, 0, 0))
    return pl.pallas_call(
        _gla_kernel,
        grid=(nb, nl),
        in_specs=[row(GLA_QK), row(GLA_QK), row(GLA_QK), row(GLA_V), row(GLA_V),
                  full((tl, tl)), full((tl, tl)), full((GLA_QK, GLA_V)), full((GLA_QK, GLA_V)),
                  full((1, GLA_DV)), st],
        out_specs=[row(GLA_V), st],
        out_shape=[jax.ShapeDtypeStruct((t, GLA_V), BF16),
                   jax.ShapeDtypeStruct((nb, GLA_HEADS, GLA_DK, GLA_DV), F32)],
        scratch_shapes=[pltpu.VMEM((GLA_QK, GLA_V), F32), pltpu.VMEM((tl, GLA_QK), F32),
                        pltpu.VMEM((tl, GLA_QK), F32), pltpu.VMEM((GLA_QK, tl), F32),
                        pltpu.VMEM((GLA_QK, tl), F32), pltpu.VMEM((tl, GLA_V), BF16),
                        pltpu.VMEM((tl, GLA_V), F32), pltpu.VMEM((GLA_CHUNK * GLA_CHUNK, GLA_QK), BF16)],
        compiler_params=_params(("parallel", "arbitrary")),
        name="gla_prompt",
    )(gq, gk, la, gv, gr, tri, last, bd.astype(BF16), bd.astype(F32), gla_g, s0)


def _gla_step_kernel(q_ref, k_ref, la_ref, v_ref, gr_ref, g_ref, s_ref, o_ref, snew_ref):
    for h in range(GLA_HEADS):
        dk = slice(h * GLA_DK, (h + 1) * GLA_DK)
        dv = slice(h * GLA_DV, (h + 1) * GLA_DV)
        a = jnp.exp(la_ref[0, dk, :])
        s_new = a * s_ref[0, h] + k_ref[0, dk, :] * v_ref[0, :, dv]
        snew_ref[0, h] = s_new
        o = jnp.sum(q_ref[0, dk, :] * s_new, axis=0, keepdims=True)
        o_ref[0, :, dv] = _rms(o, g_ref[...]) * _silu(gr_ref[0, :, dv])


def _gla_step(gq, gk, la, gv, gr, gla_g, state):
    n = gq.shape[0]
    col = pl.BlockSpec((1, GLA_QK, 1), lambda b: (b, 0, 0))
    rowv = pl.BlockSpec((1, 1, GLA_V), lambda b: (b, 0, 0))
    st = pl.BlockSpec((1, GLA_HEADS, GLA_DK, GLA_DV), lambda b: (b, 0, 0, 0))
    return pl.pallas_call(
        _gla_step_kernel,
        grid=(n,),
        in_specs=[col, col, col, rowv, rowv, pl.BlockSpec((1, GLA_DV), lambda b: (0, 0)), st],
        out_specs=[rowv, st],
        out_shape=[jax.ShapeDtypeStruct((n, 1, GLA_V), F32),
                   jax.ShapeDtypeStruct((n, GLA_HEADS, GLA_DK, GLA_DV), F32)],
        compiler_params=_params(("parallel",)),
        name="gla_step",
    )(gq.reshape(n, GLA_QK, 1), gk.reshape(n, GLA_QK, 1), la.reshape(n, GLA_QK, 1),
      gv.reshape(n, 1, GLA_V), gr.reshape(n, 1, GLA_V), gla_g, state)


def _lambda(lq1_ref, lk1_ref, lq2_ref, lk2_ref, lam_init):
    e1 = jnp.exp(jnp.sum(lq1_ref[...] * lk1_ref[...], axis=-1, keepdims=True))
    e2 = jnp.exp(jnp.sum(lq2_ref[...] * lk2_ref[...], axis=-1, keepdims=True))
    return e1 - e2 + lam_init


def _softmax_step(s, v, m, l, acc):
    m_new = jnp.maximum(m, jnp.max(s, axis=-1, keepdims=True))
    alpha = jnp.exp2(m - m_new)
    p = jnp.exp2(s - m_new)
    l = alpha * l + jnp.sum(p, axis=-1, keepdims=True)
    acc = alpha * acc + _dot(p.astype(BF16), v)
    return m_new, l, acc


def _attn_kernel(lq1_ref, lk1_ref, lq2_ref, lk2_ref, q_ref, k_ref, vt_ref, g_ref, o_ref, s_s, a_s,
                 *, lam_init, tq, ks):
    seq = q_ref.shape[0]
    lane = lax.broadcasted_iota(jnp.int32, (tq, LANES), 1)
    rel = lax.broadcasted_iota(jnp.int32, (ks, tq), 0) - lax.broadcasted_iota(jnp.int32, (ks, tq), 1)
    lam = _lambda(lq1_ref, lk1_ref, lq2_ref, lk2_ref, lam_init)
    n_mask = max(1, tq // ks)

    def q_tile(i, _):
        q0 = pl.multiple_of(i * tq, tq)
        q = q_ref[pl.ds(q0, tq), :]
        zero = jnp.zeros_like(q)
        qs = (jnp.where(lane < DIFF_HEAD_DIM, q, zero), jnp.where(lane >= DIFF_HEAD_DIM, q, zero))
        a_s[...] = jnp.zeros_like(a_s)
        n_full = (q0 + 1) // ks
        last = n_full + n_mask - 1

        def score(j, slot):
            k = k_ref[pl.ds(pl.multiple_of(j * ks, ks), ks), :]
            for mp in range(2):
                s_s[slot, mp] = _dot_nt(k, qs[mp])

        def step(j, cur, ms, masked):
            score(jnp.minimum(j + 1, last), 1 - cur)
            k0 = pl.multiple_of(j * ks, ks)
            vt = vt_ref[0, 0, :, pl.ds(k0, ks)]
            out = []
            for mp in range(2):
                s = s_s[cur, mp]
                if masked:
                    s = jnp.where(rel <= q0 - k0, s, -jnp.inf)
                m_new = jnp.maximum(ms[mp], jnp.max(s, axis=0, keepdims=True))
                p = jnp.exp2(s - m_new).astype(BF16)
                a_s[mp] = jnp.exp2(ms[mp] - m_new) * a_s[mp] + _dot(vt, p)
                out.append(m_new)
            return tuple(out)

        minf = jnp.full((1, tq), -jnp.inf, F32)
        score(0, 0)

        def pair(jj, ms):
            return step(2 * jj + 1, 1, step(2 * jj, 0, ms, False), False)

        ms = lax.fori_loop(0, n_full // 2, pair, (minf, minf))
        for r in range(n_mask):
            ms = step(n_full + r, r % 2, ms, True)
        a1, a2 = a_s[0], a_s[1]
        w = a1[:DIFF_V_DIM] / a1[DIFF_V_DIM:DIFF_V_DIM + 1] - lam * (a2[:DIFF_V_DIM] / a2[DIFF_V_DIM:DIFF_V_DIM + 1])
        o_ref[pl.ds(q0, tq), :] = (_rms(w.T, g_ref[...]) * (1.0 - lam_init)).astype(o_ref.dtype)
        return 0

    lax.fori_loop(0, seq // tq, q_tile, 0)


def _attn_prompt(lams, dqb, dkb, vt, diff_g, nb, seq, lam_init, tq=1024, ks=512):
    t = dqb.shape[0]
    assert tq % (2 * ks) == 0 and seq % tq == 0
    lam_spec = pl.BlockSpec((1, DIFF_HEAD_DIM), lambda b, h: (0, 0))
    seqspec = pl.BlockSpec((seq, LANES), lambda b, h: (b, h))
    return pl.pallas_call(
        functools.partial(_attn_kernel, lam_init=lam_init, tq=tq, ks=ks),
        grid=(nb, DIFF_HEADS),
        in_specs=[lam_spec] * 4 + [seqspec, seqspec, pl.BlockSpec((1, 1, VT_ROWS, seq), lambda b, h: (b, h, 0, 0)),
                                  pl.BlockSpec((1, DIFF_V_DIM), lambda b, h: (0, 0))],
        out_specs=seqspec,
        out_shape=jax.ShapeDtypeStruct((t, DIFF_V), BF16),
        scratch_shapes=[pltpu.VMEM((2, 2, ks, tq), F32), pltpu.VMEM((2, VT_ROWS, tq), F32)],
        compiler_params=_params(("parallel", "parallel")),
        name="attn_prompt",
    )(*lams, dqb, dkb, vt, diff_g)


def _decode_kernel(pt_ref, lq1_ref, lk1_ref, lq2_ref, lk2_ref, q_ref, kn_ref, vn_ref, g_ref, ck_ref, cv_ref,
                   o_ref, kbuf, vbuf, sem, m_s, l_s, acc_s, *, lam_init, ppc, nch, nseq, page_rows):
    b = pl.program_id(0)
    c = pl.program_id(1)
    step = b * nch + c
    slot = lax.rem(step, 2)
    nrow = 2 * DIFF_HEADS

    def copies(bb, cc, sl):
        out = []
        for p in range(ppc):
            page = pt_ref[bb, cc * ppc + p]
            dst = pl.ds(p * page_rows, page_rows)
            out.append(pltpu.make_async_copy(ck_ref.at[page], kbuf.at[sl, dst, :], sem.at[0, sl]))
            out.append(pltpu.make_async_copy(cv_ref.at[page], vbuf.at[sl, dst, :], sem.at[1, sl]))
        return out

    @pl.when(step == 0)
    def _():
        for cp in copies(b, c, slot):
            cp.start()

    @pl.when(step + 1 < nseq * nch)
    def _():
        nxt = step + 1
        for cp in copies(nxt // nch, lax.rem(nxt, nch), 1 - slot):
            cp.start()

    @pl.when(c == 0)
    def _():
        m_s[...] = jnp.full_like(m_s, -jnp.inf)
        l_s[...] = jnp.zeros_like(l_s)
        acc_s[...] = jnp.zeros_like(acc_s)

    for cp in copies(b, c, slot):
        cp.wait()

    rowi = lax.broadcasted_iota(jnp.int32, (nrow, LANES), 0)
    lane = lax.broadcasted_iota(jnp.int32, (nrow, LANES), 1)
    q4 = q_ref[0].astype(F32)
    q8 = jnp.concatenate([q4, q4], axis=0)
    qm = jnp.where((rowi < DIFF_HEADS) == (lane < DIFF_HEAD_DIM), q8, 0.0)

    keys = ppc * page_rows
    s = _dot_nt(qm.astype(BF16), kbuf[slot].astype(BF16))
    col = lax.broadcasted_iota(jnp.int32, (nrow, keys), 1)
    rowk = lax.broadcasted_iota(jnp.int32, (nrow, keys), 0)
    s = jnp.where((col % DIFF_HEADS) == (rowk % DIFF_HEADS), s, -jnp.inf)
    m, l, acc = _softmax_step(s, vbuf[slot].astype(BF16), m_s[...], l_s[...], acc_s[...])
    m_s[...] = m
    l_s[...] = l
    acc_s[...] = acc

    @pl.when(c == nch - 1)
    def _():
        lam = _lambda(lq1_ref, lk1_ref, lq2_ref, lk2_ref, lam_init)
        kn = kn_ref[0].astype(BF16).astype(F32)
        vn = vn_ref[0].astype(BF16).astype(F32)
        sn = jnp.sum(qm * jnp.concatenate([kn, kn], axis=0), axis=-1, keepdims=True)
        m_new = jnp.maximum(m, sn)
        alpha = jnp.exp2(m - m_new)
        p = jnp.exp2(sn - m_new)
        lf = alpha * l + p
        w = (alpha * acc + p.astype(BF16).astype(F32) * jnp.concatenate([vn, vn], axis=0)) / lf
        out = w[:DIFF_HEADS] - lam * w[DIFF_HEADS:]
        o_ref[0] = _rms(out, g_ref[...]) * (1.0 - lam_init)


def _attn_decode(page_table, lams, dqb, dk, dv, diff_g, cache_k, cache_v, lam_init, ppc):
    n, n_pages = page_table.shape
    nch = n_pages // ppc
    n_pool, page = cache_k.shape[1], cache_k.shape[2]
    page_rows = page * DIFF_HEADS
    ck = cache_k.reshape(n_pool, page_rows, 2 * DIFF_HEAD_DIM)
    cv = cache_v.reshape(n_pool, page_rows, DIFF_V_DIM)
    lam_spec = pl.BlockSpec((1, DIFF_HEAD_DIM), lambda b, c, pt: (0, 0))
    rowspec = pl.BlockSpec((1, DIFF_HEADS, LANES), lambda b, c, pt: (b, 0, 0))
    anyspec = pl.BlockSpec(memory_space=pl.ANY)
    kern = functools.partial(_decode_kernel, lam_init=lam_init, ppc=ppc, nch=nch, nseq=n, page_rows=page_rows)
    nrow = 2 * DIFF_HEADS
    return pl.pallas_call(
        kern,
        grid_spec=pltpu.PrefetchScalarGridSpec(
            num_scalar_prefetch=1,
            grid=(n, nch),
            in_specs=[lam_spec] * 4 + [rowspec, rowspec, rowspec,
                                      pl.BlockSpec((1, DIFF_V_DIM), lambda b, c, pt: (0, 0)), anyspec, anyspec],
            out_specs=rowspec,
            scratch_shapes=[pltpu.VMEM((2, ppc * page_rows, LANES), F32),
                            pltpu.VMEM((2, ppc * page_rows, LANES), F32),
                            pltpu.SemaphoreType.DMA((2, 2)),
                            pltpu.VMEM((nrow, 1), F32), pltpu.VMEM((nrow, 1), F32),
                            pltpu.VMEM((nrow, DIFF_V_DIM), F32)]),
        out_shape=jax.ShapeDtypeStruct((n, DIFF_HEADS, LANES), F32),
        compiler_params=_params(("arbitrary", "arbitrary")),
        name="attn_decode",
    )(page_table, *lams, dqb.reshape(n, DIFF_HEADS, LANES), dk.reshape(n, DIFF_HEADS, LANES),
      dv.reshape(n, DIFF_HEADS, LANES), diff_g, ck, cv)


def _post_kernel(x_ref, go_ref, do_ref, wo_ref, g2_ref, wr_ref, br_ref, x1_ref, t_ref, gates_ref):
    x1 = x_ref[...] + _dot(go_ref[...].astype(BF16), wo_ref[:GLA_V, :]) + _dot(do_ref[...].astype(BF16), wo_ref[GLA_V:, :])
    x1_ref[...] = x1
    t = _rms(x1, g2_ref[...])
    t_ref[...] = t.astype(BF16)
    wr = wr_ref[...]
    t_hi = t.astype(BF16)
    t_lo = (t - t_hi.astype(F32)).astype(BF16)
    w_hi = wr.astype(BF16)
    w_lo = (wr - w_hi.astype(F32)).astype(BF16)
    logits = (_dot(t_lo, w_hi) + _dot(t_hi, w_lo)) + _dot(t_hi, w_hi) + br_ref[...]
    lg = logits[:, :N_GROUPS]
    le_all = logits[:, N_GROUPS:N_GROUPS + N_EXPERTS]
    gi = lax.broadcasted_iota(jnp.int32, lg.shape, 1)
    lg_max = jnp.max(lg, axis=-1, keepdims=True)
    g = jnp.min(jnp.where(lg == lg_max, gi, N_GROUPS), axis=-1, keepdims=True)
    p_top = 1.0 / jnp.sum(jnp.exp(lg - lg_max), axis=-1, keepdims=True)
    ei = lax.broadcasted_iota(jnp.int32, le_all.shape, 1)
    in_g = (ei // EXPERTS_PER_GROUP) == g
    neg = -jnp.inf
    le = jnp.where(in_g, le_all, neg)
    v1 = jnp.max(le, axis=-1, keepdims=True)
    i1 = jnp.min(jnp.where(le == v1, ei, N_EXPERTS), axis=-1, keepdims=True)
    le2 = jnp.where(ei == i1, neg, le)
    v2 = jnp.max(le2, axis=-1, keepdims=True)
    i2 = jnp.min(jnp.where(le2 == v2, ei, N_EXPERTS), axis=-1, keepdims=True)
    e2 = jnp.exp(v2 - v1)
    w1 = p_top / (1.0 + e2)
    w2 = p_top * e2 / (1.0 + e2)
    gates_ref[...] = jnp.where(ei == i1, w1, 0.0) + jnp.where(ei == i2, w2, 0.0)


def _post(x, gla_o, diff_o, wo, g2, wr, br, tm):
    t = x.shape[0]
    row = lambda w: pl.BlockSpec((tm, w), lambda i: (i, 0))
    full = lambda a: pl.BlockSpec(a.shape, lambda i: (0,) * a.ndim)
    return pl.pallas_call(
        _post_kernel,
        grid=(t // tm,),
        in_specs=[row(D_MODEL), row(GLA_V), row(DIFF_V), full(wo), full(g2), full(wr), full(br)],
        out_specs=[row(D_MODEL), row(D_MODEL), row(N_EXPERTS)],
        out_shape=[jax.ShapeDtypeStruct((t, D_MODEL), F32), jax.ShapeDtypeStruct((t, D_MODEL), BF16),
                   jax.ShapeDtypeStruct((t, N_EXPERTS), F32)],
        compiler_params=_params(("parallel",)),
        name="post",
    )(x, gla_o, diff_o, wo, g2, wr, br)


def _moe_kernel(t_ref, gates_ref, x1_ref, wgu_ref, wd_ref, gf_ref, y_ref, acc_s, *, eps_per_step):
    e0 = pl.program_id(1)

    @pl.when(e0 == 0)
    def _():
        acc_s[...] = jnp.zeros_like(acc_s)

    t = t_ref[...]
    gates = gates_ref[...]
    ei = lax.broadcasted_iota(jnp.int32, gates.shape, 1)
    acc = acc_s[...]
    for k in range(eps_per_step):
        gu = _dot(t, wgu_ref[k])
        gate = jnp.sum(jnp.where(ei == e0 * eps_per_step + k, gates, 0.0), axis=-1, keepdims=True)
        hid = _silu(gu[:, :EXPERT_FF]) * gu[:, EXPERT_FF:] * gate
        acc = acc + _dot(hid.astype(BF16), wd_ref[k])
    acc_s[...] = acc

    @pl.when(e0 == pl.num_programs(1) - 1)
    def _():
        y_ref[...] = _rms(x1_ref[...] + acc_s[...], gf_ref[...])


def _moe(t, gates, x1, wgu, wd, gf, tm, eps_per_step):
    n = t.shape[0]
    row = lambda w: pl.BlockSpec((tm, w), lambda i, e: (i, 0))
    return pl.pallas_call(
        functools.partial(_moe_kernel, eps_per_step=eps_per_step),
        grid=(n // tm, N_EXPERTS // eps_per_step),
        in_specs=[row(D_MODEL), row(N_EXPERTS), row(D_MODEL),
                  pl.BlockSpec((eps_per_step, D_MODEL, 2 * EXPERT_FF), lambda i, e: (e, 0, 0)),
                  pl.BlockSpec((eps_per_step, EXPERT_FF, D_MODEL), lambda i, e: (e, 0, 0)),
                  pl.BlockSpec((1, D_MODEL), lambda i, e: (0, 0))],
        out_specs=row(D_MODEL),
        out_shape=jax.ShapeDtypeStruct((n, D_MODEL), F32),
        scratch_shapes=[pltpu.VMEM((tm, D_MODEL), F32)],
        compiler_params=_params(("parallel", "arbitrary")),
        name="moe",
    )(t, gates, x1, wgu, wd, gf)


def kernel(x_prompt, x_sample, cache_k, cache_v, state_gla, page_table, norm1_g, w_in, w_a2, b_a, gla_norm_g,
           lambda_q1, lambda_k1, lambda_q2, lambda_k2, diff_norm_g, w_out, norm2_g, w_router_g, b_router_g,
           w_router_e, b_router_e, w_gate, w_up, w_down, norm_f_g):
    nb, seq, _ = x_prompt.shape
    ns = x_sample.shape[0]
    depth = w_in.shape[0]
    assert depth == 1 and x_sample.shape[1] == 1
    lam_init = 0.8 - 0.6 * math.exp(-0.3 * 0)

    c_code = 2 * GLA_QK + 2 * GLA_V
    w = w_in[0]
    wm = jnp.concatenate([w[:, :c_code], w[:, c_code + GLA_RANK:]], axis=1).astype(BF16)
    wc = jnp.pad(w[:, c_code:c_code + GLA_RANK], ((0, 0), (0, LANES - GLA_RANK))).astype(BF16)
    wa2p = jnp.pad(w_a2[0], ((0, LANES - GLA_RANK), (0, 0))).astype(BF16)
    ba = b_a[0].reshape(1, GLA_QK)
    g1 = norm1_g[0].reshape(1, D_MODEL)
    g2 = norm2_g[0].reshape(1, D_MODEL)
    gf = norm_f_g.reshape(1, D_MODEL)
    gla_g = gla_norm_g[0].reshape(1, GLA_DV)
    diff_g = diff_norm_g[0].reshape(1, DIFF_V_DIM)
    lams = [a[0].reshape(1, DIFF_HEAD_DIM) for a in (lambda_q1, lambda_k1, lambda_q2, lambda_k2)]
    wo = w_out[0].astype(BF16)
    wr = jnp.concatenate([w_router_g[0], w_router_e[0].transpose(1, 0, 2).reshape(D_MODEL, N_EXPERTS)], axis=1)
    wr = jnp.pad(wr, ((0, 0), (0, LANES - N_GROUPS - N_EXPERTS)))
    br = jnp.pad(jnp.concatenate([b_router_g[0], b_router_e[0].reshape(N_EXPERTS)]),
                 (0, LANES - N_GROUPS - N_EXPERTS)).reshape(1, LANES)
    wgu = jnp.concatenate([w_gate[0], w_up[0]], axis=-1).astype(BF16)
    wd = w_down[0].astype(BF16)

    def dense_tail(x, gla_o, diff_o, tm_post, tm_moe):
        x1, t, gates = _post(x, gla_o, diff_o, wo, g2, wr, br, tm_post)
        return _moe(t, gates, x1, wgu, wd, gf, tm_moe, EXPERTS_PER_GROUP)

    xp = x_prompt.reshape(nb * seq, D_MODEL)
    gq, gk, la, gv, gr, dqb, dk, dv, dkb, vt = _inproj(xp, g1, wm, wc, wa2p, ba, 256, (nb, seq))
    s0 = jnp.zeros((nb, GLA_HEADS, GLA_DK, GLA_DV), F32)
    gla_o, s_prompt = _gla_prompt(gq, gk, la, gv, gr, gla_g, s0, nb, 256)
    diff_o = _attn_prompt(lams, dqb, dkb, vt, diff_g, nb, seq, lam_init)
    y_prompt = dense_tail(xp, gla_o, diff_o, 256, 1024).reshape(nb, seq, D_MODEL)

    xs = x_sample.reshape(ns, D_MODEL)
    sgq, sgk, sla, sgv, sgr, sdqb, sdk, sdv = _inproj(xs, g1, wm, wc, wa2p, ba, ns)
    sgla_o, s_sample = _gla_step(sgq, sgk, sla, sgv, sgr, gla_g, state_gla[0])
    sdiff_o = _attn_decode(page_table, lams, sdqb, sdk, sdv, diff_g, cache_k, cache_v, lam_init, 8)
    y_sample = dense_tail(xs, sgla_o.reshape(ns, GLA_V), sdiff_o.reshape(ns, DIFF_V), ns, ns).reshape(ns, 1, D_MODEL)

    k_prompt = dk.reshape(1, nb, seq, DIFF_HEADS, 2 * DIFF_HEAD_DIM)
    v_prompt = dv.reshape(1, nb, seq, DIFF_HEADS, DIFF_V_DIM)
    k_sample = sdk.reshape(1, ns, 1, DIFF_HEADS, 2 * DIFF_HEAD_DIM)
    v_sample = sdv.reshape(1, ns, 1, DIFF_HEADS, DIFF_V_DIM)
    return (y_prompt, y_sample, k_prompt, v_prompt, s_prompt[None], k_sample, v_sample, s_sample[None])
```

```python
import functools
import math

import jax
import jax.numpy as jnp
from jax import lax
from jax.experimental import pallas as pl
from jax.experimental.pallas import tpu as pltpu

F32 = jnp.float32
BF16 = jnp.bfloat16

D_MODEL = 1024
GLA_HEADS = 4
GLA_DK = 64
GLA_DV = 128
GLA_RANK = 16
GLA_TAU = 16.0
GLA_CHUNK = 16
DIFF_HEADS = 4
DIFF_HEAD_DIM = 64
DIFF_V_DIM = 128
GLA_QK = GLA_HEADS * GLA_DK
GLA_V = GLA_HEADS * GLA_DV
DIFF_QK = DIFF_HEADS * 2 * DIFF_HEAD_DIM
DIFF_V = DIFF_HEADS * DIFF_V_DIM
N_GROUPS = 4
EXPERTS_PER_GROUP = 4
N_EXPERTS = 16
EXPERT_FF = 256
EPS = 1e-6
LANES = 128
BF16_SUBLANES = 16
LOG2E = math.log2(math.e)
DECODE_SLOTS = 3
VT_ROWS = DIFF_V_DIM + BF16_SUBLANES
VMEM_LIMIT = 56 * 1024 * 1024

_C_GQ, _C_GK, _C_GV, _C_GR, _C_DQ, _C_DK, _C_DV, _C_END = 0, 256, 512, 1024, 1536, 2048, 2560, 3072


def _rms(x, g):
    return x * lax.rsqrt(jnp.mean(x * x, axis=-1, keepdims=True) + EPS) * g


def _silu(x):
    return x * (1.0 / (1.0 + jnp.exp(-x)))


def _log_sigmoid(x):
    return jnp.minimum(x, 0.0) - jnp.log1p(jnp.exp(-jnp.abs(x)))


def _dot(a, b):
    return jnp.dot(a, b, preferred_element_type=F32)


def _dot_nt(a, b):
    return lax.dot_general(a, b, (((1,), (1,)), ((), ())), preferred_element_type=F32)


def _split3(x):
    hi = x.astype(BF16)
    r = x - hi.astype(F32)
    mid = r.astype(BF16)
    lo = (r - mid.astype(F32)).astype(BF16)
    return hi, mid, lo


def _params(sem):
    return pltpu.CompilerParams(dimension_semantics=sem, vmem_limit_bytes=VMEM_LIMIT)


def _inproj_kernel(x_ref, g_ref, wm_ref, wc_ref, wa2_ref, ba_ref,
                   gq_ref, gk_ref, la_ref, gv_ref, gr_ref, dqb_ref, dk_ref, dv_ref, *prompt_refs):
    h = _rms(x_ref[...], g_ref[...]).astype(BF16)

    def seg(lo, hi):
        return _dot(h, wm_ref[:, lo:hi])

    gq_ref[...] = seg(_C_GQ, _C_GK) * (GLA_DK ** -0.5)
    gk_ref[...] = seg(_C_GK, _C_GV)
    gv_ref[...] = seg(_C_GV, _C_GR)
    gr_ref[...] = seg(_C_GR, _C_DQ)
    dqb_ref[...] = (seg(_C_DQ, _C_DK) * (DIFF_HEAD_DIM ** -0.5 * LOG2E)).astype(BF16)
    dk = seg(_C_DK, _C_DV)
    dv = seg(_C_DV, _C_END)
    tm = dk.shape[0]
    for hd in range(DIFF_HEADS):
        rows = pl.ds(hd, tm, stride=DIFF_HEADS)
        dk_ref[rows, :] = dk[:, hd * LANES:(hd + 1) * LANES]
        dv_ref[rows, :] = dv[:, hd * LANES:(hd + 1) * LANES]
    if prompt_refs:
        dkb_ref, vt_ref = prompt_refs
        dkb_ref[...] = dk.astype(BF16)
        ones_row = (lax.broadcasted_iota(jnp.int32, (VT_ROWS - DIFF_V_DIM, tm), 0) == 0).astype(F32)
        for hd in range(DIFF_HEADS):
            vt_ref[0, hd, :DIFF_V_DIM, :] = dv[:, hd * DIFF_V_DIM:(hd + 1) * DIFF_V_DIM].T.astype(BF16)
            vt_ref[0, hd, DIFF_V_DIM:, :] = ones_row.astype(BF16)
    code = _dot(h, wc_ref[...])
    pre = _dot(code.astype(BF16), wa2_ref[...]) + ba_ref[...]
    la_ref[...] = _log_sigmoid(pre) * (1.0 / GLA_TAU)


def _inproj(x, g1, wm, wc, wa2p, ba, tm, prompt_shape=None):
    t = x.shape[0]
    row = lambda w: pl.BlockSpec((tm, w), lambda i: (i, 0))
    full = lambda a: pl.BlockSpec(a.shape, lambda i: (0,) * a.ndim)
    outs = [(GLA_QK, F32), (GLA_QK, F32), (GLA_QK, F32), (GLA_V, F32), (GLA_V, F32), (DIFF_QK, BF16)]
    out_specs = [row(w) for w, _ in outs]
    out_shape = [jax.ShapeDtypeStruct((t, w), dt) for w, dt in outs]
    out_specs += [pl.BlockSpec((tm * DIFF_HEADS, LANES), lambda i: (i, 0))] * 2
    out_shape += [jax.ShapeDtypeStruct((t * DIFF_HEADS, LANES), F32)] * 2
    if prompt_shape is not None:
        nb, seq = prompt_shape
        nl = seq // tm
        out_specs += [row(DIFF_QK), pl.BlockSpec((1, DIFF_HEADS, VT_ROWS, tm), lambda i: (i // nl, 0, 0, i % nl))]
        out_shape += [jax.ShapeDtypeStruct((t, DIFF_QK), BF16),
                      jax.ShapeDtypeStruct((nb, DIFF_HEADS, VT_ROWS, seq), BF16)]
    return pl.pallas_call(
        _inproj_kernel,
        grid=(t // tm,),
        in_specs=[row(D_MODEL), full(g1), full(wm), full(wc), full(wa2p), full(ba)],
        out_specs=out_specs,
        out_shape=out_shape,
        compiler_params=_params(("parallel",)),
        name="inproj",
    )(x, g1, wm, wc, wa2p, ba)


def _dot_tn(a, b):
    return lax.dot_general(a, b, (((0,), (0,)), ((), ())), preferred_element_type=F32)


def _gla_kernel(gq_ref, gk_ref, la_ref, gv_ref, gr_ref, tri_ref, last_ref, csel_ref, ind_ref, g_ref, s0_ref,
                o_ref, sfin_ref,
                s_s, b_s, qi_s, kd_s, adec_s, o_s, p_s):
    l = pl.program_id(1)
    tl = gq_ref.shape[0]
    c16 = GLA_CHUNK

    @pl.when(l == 0)
    def _():
        for h in range(GLA_HEADS):
            s_s[h * GLA_DK:(h + 1) * GLA_DK, :] = s0_ref[0, h]

    hi, mid, lo = _split3(la_ref[...])

    def cs(m):
        return (_dot(m, lo) + _dot(m, mid)) + _dot(m, hi)

    b = cs(tri_ref[...])
    bl = cs(last_ref[...])
    b_s[...] = b
    qi_s[...] = gq_ref[...] * jnp.exp(b)
    kd_s[...] = (gk_ref[...] * jnp.exp(bl - b)).astype(BF16)
    csel = csel_ref[...]
    adec_s[...] = jnp.exp((_dot_tn(lo, csel) + _dot_tn(mid, csel)) + _dot_tn(hi, csel))

    rowi = lax.broadcasted_iota(jnp.int32, (c16, GLA_QK), 0)
    lane_c = lax.broadcasted_iota(jnp.int32, (GLA_QK, LANES), 1)

    def chunk(c, carry):
        r0 = pl.multiple_of(c * c16, c16)
        q_c = gq_ref[pl.ds(r0, c16), :]
        b_c = b_s[pl.ds(r0, c16), :]
        for j in range(c16):
            bj = b_s[pl.ds(r0 + j, 1), :]
            kj = gk_ref[pl.ds(r0 + j, 1), :]
            e = jnp.exp(jnp.where(rowi >= j, b_c - bj, -jnp.inf))
            p_s[j * c16:(j + 1) * c16, :] = (q_c * e * kj).astype(BF16)
        r = _dot(p_s[...], ind_ref[...])
        qi_c = qi_s[pl.ds(r0, c16), :].astype(BF16)
        kd_c = kd_s[pl.ds(r0, c16), :]
        v_c = gv_ref[pl.ds(r0, c16), :]
        vb_c = v_c.astype(BF16)
        a_col = jnp.sum(jnp.where(lane_c == c, adec_s[...], 0.0), axis=1, keepdims=True)
        o_parts = []
        for h in range(GLA_HEADS):
            dk = slice(h * GLA_DK, (h + 1) * GLA_DK)
            dv = slice(h * GLA_DV, (h + 1) * GLA_DV)
            s_h = s_s[dk, :]
            o_parts.append(_dot(qi_c[:, dk], s_h.astype(BF16)))
            s_s[dk, :] = a_col[dk] * s_h + _dot_tn(kd_c[:, dk], vb_c[:, dv])
        o = jnp.concatenate(o_parts, axis=1)
        for j in range(c16):
            o = o + r[j * c16:(j + 1) * c16, :] * gv_ref[pl.ds(r0 + j, 1), :]
        o_s[pl.ds(r0, c16), :] = o
        return carry

    lax.fori_loop(0, tl // c16, chunk, 0, unroll=8)

    o = o_s[...]
    for h in range(GLA_HEADS):
        sl = slice(h * GLA_DV, (h + 1) * GLA_DV)
        o_ref[:, sl] = (_rms(o[:, sl], g_ref[...]) * _silu(gr_ref[:, sl])).astype(o_ref.dtype)

    @pl.when(l == pl.num_programs(1) - 1)
    def _():
        for h in range(GLA_HEADS):
            sfin_ref[0, h] = s_s[h * GLA_DK:(h + 1) * GLA_DK, :]


def _gla_prompt(gq, gk, la, gv, gr, gla_g, s0, nb, tl):
    t = gq.shape[0]
    nl = t // nb // tl
    ti = jnp.arange(tl)
    same = (ti[:, None] // GLA_CHUNK) == (ti[None, :] // GLA_CHUNK)
    tri = (same & (ti[None, :] <= ti[:, None])).astype(BF16)
    last = same.astype(BF16)
    csel = ((ti[:, None] // GLA_CHUNK) == jnp.arange(LANES)[None, :]).astype(BF16)
    hq = jnp.arange(GLA_QK) // GLA_DK
    hv = jnp.arange(GLA_V) // GLA_DV
    bd = hq[:, None] == hv[None, :]
    row = lambda w: pl.BlockSpec((tl, w), lambda b, l: (b * nl + l, 0))
    full = lambda shp: pl.BlockSpec(shp, lambda b, l: (0,) * len(shp))
    st = pl.BlockSpec((1, GLA_HEADS, GLA_DK, GLA_DV), lambda b, l: (b, 0, 0, 0))
    return pl.pallas_call(
        _gla_kernel,
        grid=(nb, nl),
        in_specs=[row(GLA_QK), row(GLA_QK), row(GLA_QK), row(GLA_V), row(GLA_V),
                  full((tl, tl)), full((tl, tl)), full((tl, LANES)), full((GLA_QK, GLA_V)),
                  full((1, GLA_DV)), st],
        out_specs=[row(GLA_V), st],
        out_shape=[jax.ShapeDtypeStruct((t, GLA_V), BF16),
                   jax.ShapeDtypeStruct((nb, GLA_HEADS, GLA_DK, GLA_DV), F32)],
        scratch_shapes=[pltpu.VMEM((GLA_QK, GLA_DV), F32), pltpu.VMEM((tl, GLA_QK), F32),
                        pltpu.VMEM((tl, GLA_QK), F32), pltpu.VMEM((tl, GLA_QK), BF16),
                        pltpu.VMEM((GLA_QK, LANES), F32),
                        pltpu.VMEM((tl, GLA_V), F32), pltpu.VMEM((GLA_CHUNK * GLA_CHUNK, GLA_QK), BF16)],
        compiler_params=_params(("parallel", "arbitrary")),
        name="gla_prompt",
    )(gq, gk, la, gv, gr, tri, last, csel, bd.astype(BF16), gla_g, s0)


def _gla_step_kernel(q_ref, k_ref, la_ref, v_ref, gr_ref, g_ref, s_ref, o_ref, snew_ref):
    for h in range(GLA_HEADS):
        dk = slice(h * GLA_DK, (h + 1) * GLA_DK)
        dv = slice(h * GLA_DV, (h + 1) * GLA_DV)
        a = jnp.exp(la_ref[0, dk, :])
        s_new = a * s_ref[0, h] + k_ref[0, dk, :] * v_ref[0, :, dv]
        snew_ref[0, h] = s_new
        o = jnp.sum(q_ref[0, dk, :] * s_new, axis=0, keepdims=True)
        o_ref[0, :, dv] = _rms(o, g_ref[...]) * _silu(gr_ref[0, :, dv])


def _gla_step(gq, gk, la, gv, gr, gla_g, state):
    n = gq.shape[0]
    col = pl.BlockSpec((1, GLA_QK, 1), lambda b: (b, 0, 0))
    rowv = pl.BlockSpec((1, 1, GLA_V), lambda b: (b, 0, 0))
    st = pl.BlockSpec((1, GLA_HEADS, GLA_DK, GLA_DV), lambda b: (b, 0, 0, 0))
    return pl.pallas_call(
        _gla_step_kernel,
        grid=(n,),
        in_specs=[col, col, col, rowv, rowv, pl.BlockSpec((1, GLA_DV), lambda b: (0, 0)), st],
        out_specs=[rowv, st],
        out_shape=[jax.ShapeDtypeStruct((n, 1, GLA_V), F32),
                   jax.ShapeDtypeStruct((n, GLA_HEADS, GLA_DK, GLA_DV), F32)],
        compiler_params=_params(("parallel",)),
        name="gla_step",
    )(gq.reshape(n, GLA_QK, 1), gk.reshape(n, GLA_QK, 1), la.reshape(n, GLA_QK, 1),
      gv.reshape(n, 1, GLA_V), gr.reshape(n, 1, GLA_V), gla_g, state)


def _lambda(lq1_ref, lk1_ref, lq2_ref, lk2_ref, lam_init):
    e1 = jnp.exp(jnp.sum(lq1_ref[...] * lk1_ref[...], axis=-1, keepdims=True))
    e2 = jnp.exp(jnp.sum(lq2_ref[...] * lk2_ref[...], axis=-1, keepdims=True))
    return e1 - e2 + lam_init


def _softmax_step(s, v, m, l, acc):
    m_new = jnp.maximum(m, jnp.max(s, axis=-1, keepdims=True))
    alpha = jnp.exp2(m - m_new)
    p = jnp.exp2(s - m_new)
    l = alpha * l + jnp.sum(p, axis=-1, keepdims=True)
    acc = alpha * acc + _dot(p.astype(BF16), v)
    return m_new, l, acc


def _attn_kernel(lq1_ref, lk1_ref, lq2_ref, lk2_ref, q_ref, k_ref, vt_ref, g_ref, o_ref, s_s, a_s,
                 *, lam_init, tq, ks):
    seq = q_ref.shape[0]
    lane = lax.broadcasted_iota(jnp.int32, (tq, LANES), 1)
    rel = lax.broadcasted_iota(jnp.int32, (ks, tq), 0) - lax.broadcasted_iota(jnp.int32, (ks, tq), 1)
    lam = _lambda(lq1_ref, lk1_ref, lq2_ref, lk2_ref, lam_init)
    n_mask = max(1, tq // ks)

    def q_tile(i, _):
        q0 = pl.multiple_of(i * tq, tq)
        q = q_ref[pl.ds(q0, tq), :]
        zero = jnp.zeros_like(q)
        qs = (jnp.where(lane < DIFF_HEAD_DIM, q, zero), jnp.where(lane >= DIFF_HEAD_DIM, q, zero))
        a_s[...] = jnp.zeros_like(a_s)
        n_full = (q0 + 1) // ks
        last = n_full + n_mask - 1

        def score(j, slot):
            k = k_ref[pl.ds(pl.multiple_of(j * ks, ks), ks), :]
            for mp in range(2):
                s_s[slot, mp] = _dot_nt(k, qs[mp])

        def step(j, cur, ms, masked):
            score(jnp.minimum(j + 1, last), 1 - cur)
            k0 = pl.multiple_of(j * ks, ks)
            vt = vt_ref[0, 0, :, pl.ds(k0, ks)]
            out = []
            for mp in range(2):
                s = s_s[cur, mp]
                if masked:
                    s = jnp.where(rel <= q0 - k0, s, -jnp.inf)
                m_new = jnp.maximum(ms[mp], jnp.max(s, axis=0, keepdims=True))
                p = jnp.exp2(s - m_new).astype(BF16)
                a_s[mp] = jnp.exp2(ms[mp] - m_new) * a_s[mp] + _dot(vt, p)
                out.append(m_new)
            return tuple(out)

        minf = jnp.full((1, tq), -jnp.inf, F32)
        score(0, 0)

        def pair(jj, ms):
            return step(2 * jj + 1, 1, step(2 * jj, 0, ms, False), False)

        ms = lax.fori_loop(0, n_full // 2, pair, (minf, minf))
        for r in range(n_mask):
            ms = step(n_full + r, r % 2, ms, True)
        a1, a2 = a_s[0], a_s[1]
        w = a1[:DIFF_V_DIM] / a1[DIFF_V_DIM:DIFF_V_DIM + 1] - lam * (a2[:DIFF_V_DIM] / a2[DIFF_V_DIM:DIFF_V_DIM + 1])
        o_ref[pl.ds(q0, tq), :] = (_rms(w.T, g_ref[...]) * (1.0 - lam_init)).astype(o_ref.dtype)
        return 0

    lax.fori_loop(0, seq // tq, q_tile, 0)


def _attn_prompt(lams, dqb, dkb, vt, diff_g, nb, seq, lam_init, tq=1024, ks=512):
    t = dqb.shape[0]
    assert tq % (2 * ks) == 0 and seq % tq == 0
    lam_spec = pl.BlockSpec((1, DIFF_HEAD_DIM), lambda b, h: (0, 0))
    seqspec = pl.BlockSpec((seq, LANES), lambda b, h: (b, h))
    return pl.pallas_call(
        functools.partial(_attn_kernel, lam_init=lam_init, tq=tq, ks=ks),
        grid=(nb, DIFF_HEADS),
        in_specs=[lam_spec] * 4 + [seqspec, seqspec, pl.BlockSpec((1, 1, VT_ROWS, seq), lambda b, h: (b, h, 0, 0)),
                                  pl.BlockSpec((1, DIFF_V_DIM), lambda b, h: (0, 0))],
        out_specs=seqspec,
        out_shape=jax.ShapeDtypeStruct((t, DIFF_V), BF16),
        scratch_shapes=[pltpu.VMEM((2, 2, ks, tq), F32), pltpu.VMEM((2, VT_ROWS, tq), F32)],
        compiler_params=_params(("parallel", "parallel")),
        name="attn_prompt",
    )(*lams, dqb, dkb, vt, diff_g)


def _decode_kernel(pt_ref, lq1_ref, lk1_ref, lq2_ref, lk2_ref, q_ref, kn_ref, vn_ref, g_ref, ck_ref, cv_ref,
                   o_ref, kbuf, vbuf, sem, m_s, l_s, acc_s, *, lam_init, ppc, nch, nseq, page_rows):
    b = pl.program_id(0)
    c = pl.program_id(1)
    step = b * nch + c
    slot = lax.rem(step, DECODE_SLOTS)
    nrow = 2 * DIFF_HEADS

    def copies(n):
        bb, cc, sl = n // nch, lax.rem(n, nch), lax.rem(n, DECODE_SLOTS)
        out = []
        for p in range(ppc):
            page = pt_ref[bb, cc * ppc + p]
            dst = pl.ds(p * page_rows, page_rows)
            out.append(pltpu.make_async_copy(ck_ref.at[page], kbuf.at[sl, dst, :], sem.at[0, sl]))
            out.append(pltpu.make_async_copy(cv_ref.at[page], vbuf.at[sl, dst, :], sem.at[1, sl]))
        return out

    ahead = DECODE_SLOTS - 1

    @pl.when(step == 0)
    def _():
        for n in range(ahead):
            for cp in copies(jnp.int32(n)):
                cp.start()

    @pl.when(step + ahead < nseq * nch)
    def _():
        for cp in copies(step + ahead):
            cp.start()

    @pl.when(c == 0)
    def _():
        m_s[...] = jnp.full_like(m_s, -jnp.inf)
        l_s[...] = jnp.zeros_like(l_s)
        acc_s[...] = jnp.zeros_like(acc_s)

    for cp in copies(step):
        cp.wait()

    rowi = lax.broadcasted_iota(jnp.int32, (nrow, LANES), 0)
    lane = lax.broadcasted_iota(jnp.int32, (nrow, LANES), 1)
    q4 = q_ref[0].astype(F32)
    q8 = jnp.concatenate([q4, q4], axis=0)
    qm = jnp.where((rowi < DIFF_HEADS) == (lane < DIFF_HEAD_DIM), q8, 0.0)

    keys = ppc * page_rows
    s = _dot_nt(qm.astype(BF16), kbuf[slot].astype(BF16))
    col = lax.broadcasted_iota(jnp.int32, (nrow, keys), 1)
    rowk = lax.broadcasted_iota(jnp.int32, (nrow, keys), 0)
    s = jnp.where((col % DIFF_HEADS) == (rowk % DIFF_HEADS), s, -jnp.inf)
    m, l, acc = _softmax_step(s, vbuf[slot].astype(BF16), m_s[...], l_s[...], acc_s[...])
    m_s[...] = m
    l_s[...] = l
    acc_s[...] = acc

    @pl.when(c == nch - 1)
    def _():
        lam = _lambda(lq1_ref, lk1_ref, lq2_ref, lk2_ref, lam_init)
        kn = kn_ref[0].astype(BF16).astype(F32)
        vn = vn_ref[0].astype(BF16).astype(F32)
        sn = jnp.sum(qm * jnp.concatenate([kn, kn], axis=0), axis=-1, keepdims=True)
        m_new = jnp.maximum(m, sn)
        alpha = jnp.exp2(m - m_new)
        p = jnp.exp2(sn - m_new)
        lf = alpha * l + p
        w = (alpha * acc + p.astype(BF16).astype(F32) * jnp.concatenate([vn, vn], axis=0)) / lf
        out = w[:DIFF_HEADS] - lam * w[DIFF_HEADS:]
        o_ref[0] = _rms(out, g_ref[...]) * (1.0 - lam_init)


def _attn_decode(page_table, lams, dqb, dk, dv, diff_g, cache_k, cache_v, lam_init, ppc):
    n, n_pages = page_table.shape
    nch = n_pages // ppc
    n_pool, page = cache_k.shape[1], cache_k.shape[2]
    page_rows = page * DIFF_HEADS
    ck = cache_k.reshape(n_pool, page_rows, 2 * DIFF_HEAD_DIM)
    cv = cache_v.reshape(n_pool, page_rows, DIFF_V_DIM)
    lam_spec = pl.BlockSpec((1, DIFF_HEAD_DIM), lambda b, c, pt: (0, 0))
    rowspec = pl.BlockSpec((1, DIFF_HEADS, LANES), lambda b, c, pt: (b, 0, 0))
    anyspec = pl.BlockSpec(memory_space=pl.ANY)
    kern = functools.partial(_decode_kernel, lam_init=lam_init, ppc=ppc, nch=nch, nseq=n, page_rows=page_rows)
    nrow = 2 * DIFF_HEADS
    return pl.pallas_call(
        kern,
        grid_spec=pltpu.PrefetchScalarGridSpec(
            num_scalar_prefetch=1,
            grid=(n, nch),
            in_specs=[lam_spec] * 4 + [rowspec, rowspec, rowspec,
                                      pl.BlockSpec((1, DIFF_V_DIM), lambda b, c, pt: (0, 0)), anyspec, anyspec],
            out_specs=rowspec,
            scratch_shapes=[pltpu.VMEM((DECODE_SLOTS, ppc * page_rows, LANES), F32),
                            pltpu.VMEM((DECODE_SLOTS, ppc * page_rows, LANES), F32),
                            pltpu.SemaphoreType.DMA((2, DECODE_SLOTS)),
                            pltpu.VMEM((nrow, 1), F32), pltpu.VMEM((nrow, 1), F32),
                            pltpu.VMEM((nrow, DIFF_V_DIM), F32)]),
        out_shape=jax.ShapeDtypeStruct((n, DIFF_HEADS, LANES), F32),
        compiler_params=_params(("arbitrary", "arbitrary")),
        name="attn_decode",
    )(page_table, *lams, dqb.reshape(n, DIFF_HEADS, LANES), dk.reshape(n, DIFF_HEADS, LANES),
      dv.reshape(n, DIFF_HEADS, LANES), diff_g, ck, cv)


def _post_kernel(x_ref, go_ref, do_ref, wo_ref, g2_ref, wr_ref, br_ref, x1_ref, t_ref, gates_ref):
    x1 = x_ref[...] + _dot(go_ref[...].astype(BF16), wo_ref[:GLA_V, :]) + _dot(do_ref[...].astype(BF16), wo_ref[GLA_V:, :])
    x1_ref[...] = x1
    t = _rms(x1, g2_ref[...])
    t_ref[...] = t.astype(BF16)
    wr = wr_ref[...]
    t_hi = t.astype(BF16)
    t_lo = (t - t_hi.astype(F32)).astype(BF16)
    w_hi = wr.astype(BF16)
    w_lo = (wr - w_hi.astype(F32)).astype(BF16)
    logits = (_dot(t_lo, w_hi) + _dot(t_hi, w_lo)) + _dot(t_hi, w_hi) + br_ref[...]
    lg = logits[:, :N_GROUPS]
    le_all = logits[:, N_GROUPS:N_GROUPS + N_EXPERTS]
    gi = lax.broadcasted_iota(jnp.int32, lg.shape, 1)
    lg_max = jnp.max(lg, axis=-1, keepdims=True)
    g = jnp.min(jnp.where(lg == lg_max, gi, N_GROUPS), axis=-1, keepdims=True)
    p_top = 1.0 / jnp.sum(jnp.exp(lg - lg_max), axis=-1, keepdims=True)
    ei = lax.broadcasted_iota(jnp.int32, le_all.shape, 1)
    in_g = (ei // EXPERTS_PER_GROUP) == g
    neg = -jnp.inf
    le = jnp.where(in_g, le_all, neg)
    v1 = jnp.max(le, axis=-1, keepdims=True)
    i1 = jnp.min(jnp.where(le == v1, ei, N_EXPERTS), axis=-1, keepdims=True)
    le2 = jnp.where(ei == i1, neg, le)
    v2 = jnp.max(le2, axis=-1, keepdims=True)
    i2 = jnp.min(jnp.where(le2 == v2, ei, N_EXPERTS), axis=-1, keepdims=True)
    e2 = jnp.exp(v2 - v1)
    w1 = p_top / (1.0 + e2)
    w2 = p_top * e2 / (1.0 + e2)
    gates_ref[...] = jnp.where(ei == i1, w1, 0.0) + jnp.where(ei == i2, w2, 0.0)


def _post(x, gla_o, diff_o, wo, g2, wr, br, tm):
    t = x.shape[0]
    row = lambda w: pl.BlockSpec((tm, w), lambda i: (i, 0))
    full = lambda a: pl.BlockSpec(a.shape, lambda i: (0,) * a.ndim)
    return pl.pallas_call(
        _post_kernel,
        grid=(t // tm,),
        in_specs=[row(D_MODEL), row(GLA_V), row(DIFF_V), full(wo), full(g2), full(wr), full(br)],
        out_specs=[row(D_MODEL), row(D_MODEL), row(N_EXPERTS)],
        out_shape=[jax.ShapeDtypeStruct((t, D_MODEL), F32), jax.ShapeDtypeStruct((t, D_MODEL), BF16),
                   jax.ShapeDtypeStruct((t, N_EXPERTS), F32)],
        compiler_params=_params(("parallel",)),
        name="post",
    )(x, gla_o, diff_o, wo, g2, wr, br)


def _moe_kernel(t_ref, gates_ref, x1_ref, wgu_ref, wd_ref, gf_ref, y_ref, acc_s, *, eps_per_step):
    e0 = pl.program_id(1)

    @pl.when(e0 == 0)
    def _():
        acc_s[...] = jnp.zeros_like(acc_s)

    t = t_ref[...]
    gates = gates_ref[...]
    ei = lax.broadcasted_iota(jnp.int32, gates.shape, 1)
    acc = acc_s[...]
    for k in range(eps_per_step):
        gu = _dot(t, wgu_ref[k])
        gate = jnp.sum(jnp.where(ei == e0 * eps_per_step + k, gates, 0.0), axis=-1, keepdims=True)
        hid = _silu(gu[:, :EXPERT_FF]) * gu[:, EXPERT_FF:] * gate
        acc = acc + _dot(hid.astype(BF16), wd_ref[k])
    acc_s[...] = acc

    @pl.when(e0 == pl.num_programs(1) - 1)
    def _():
        y_ref[...] = _rms(x1_ref[...] + acc_s[...], gf_ref[...])


def _moe(t, gates, x1, wgu, wd, gf, tm, eps_per_step):
    n = t.shape[0]
    row = lambda w: pl.BlockSpec((tm, w), lambda i, e: (i, 0))
    return pl.pallas_call(
        functools.partial(_moe_kernel, eps_per_step=eps_per_step),
        grid=(n // tm, N_EXPERTS // eps_per_step),
        in_specs=[row(D_MODEL), row(N_EXPERTS), row(D_MODEL),
                  pl.BlockSpec((eps_per_step, D_MODEL, 2 * EXPERT_FF), lambda i, e: (e, 0, 0)),
                  pl.BlockSpec((eps_per_step, EXPERT_FF, D_MODEL), lambda i, e: (e, 0, 0)),
                  pl.BlockSpec((1, D_MODEL), lambda i, e: (0, 0))],
        out_specs=row(D_MODEL),
        out_shape=jax.ShapeDtypeStruct((n, D_MODEL), F32),
        scratch_shapes=[pltpu.VMEM((tm, D_MODEL), F32)],
        compiler_params=_params(("parallel", "arbitrary")),
        name="moe",
    )(t, gates, x1, wgu, wd, gf)


def kernel(x_prompt, x_sample, cache_k, cache_v, state_gla, page_table, norm1_g, w_in, w_a2, b_a, gla_norm_g,
           lambda_q1, lambda_k1, lambda_q2, lambda_k2, diff_norm_g, w_out, norm2_g, w_router_g, b_router_g,
           w_router_e, b_router_e, w_gate, w_up, w_down, norm_f_g):
    nb, seq, _ = x_prompt.shape
    ns = x_sample.shape[0]
    depth = w_in.shape[0]
    assert depth == 1 and x_sample.shape[1] == 1
    lam_init = 0.8 - 0.6 * math.exp(-0.3 * 0)

    c_code = 2 * GLA_QK + 2 * GLA_V
    w = w_in[0]
    wm = jnp.concatenate([w[:, :c_code], w[:, c_code + GLA_RANK:]], axis=1).astype(BF16)
    wc = jnp.pad(w[:, c_code:c_code + GLA_RANK], ((0, 0), (0, LANES - GLA_RANK))).astype(BF16)
    wa2p = jnp.pad(w_a2[0], ((0, LANES - GLA_RANK), (0, 0))).astype(BF16)
    ba = b_a[0].reshape(1, GLA_QK)
    g1 = norm1_g[0].reshape(1, D_MODEL)
    g2 = norm2_g[0].reshape(1, D_MODEL)
    gf = norm_f_g.reshape(1, D_MODEL)
    gla_g = gla_norm_g[0].reshape(1, GLA_DV)
    diff_g = diff_norm_g[0].reshape(1, DIFF_V_DIM)
    lams = [a[0].reshape(1, DIFF_HEAD_DIM) for a in (lambda_q1, lambda_k1, lambda_q2, lambda_k2)]
    wo = w_out[0].astype(BF16)
    wr = jnp.concatenate([w_router_g[0], w_router_e[0].transpose(1, 0, 2).reshape(D_MODEL, N_EXPERTS)], axis=1)
    wr = jnp.pad(wr, ((0, 0), (0, LANES - N_GROUPS - N_EXPERTS)))
    br = jnp.pad(jnp.concatenate([b_router_g[0], b_router_e[0].reshape(N_EXPERTS)]),
                 (0, LANES - N_GROUPS - N_EXPERTS)).reshape(1, LANES)
    wgu = jnp.concatenate([w_gate[0], w_up[0]], axis=-1).astype(BF16)
    wd = w_down[0].astype(BF16)

    def dense_tail(x, gla_o, diff_o, tm_post, tm_moe):
        x1, t, gates = _post(x, gla_o, diff_o, wo, g2, wr, br, tm_post)
        return _moe(t, gates, x1, wgu, wd, gf, tm_moe, EXPERTS_PER_GROUP)

    xp = x_prompt.reshape(nb * seq, D_MODEL)
    gq, gk, la, gv, gr, dqb, dk, dv, dkb, vt = _inproj(xp, g1, wm, wc, wa2p, ba, 256, (nb, seq))
    s0 = jnp.zeros((nb, GLA_HEADS, GLA_DK, GLA_DV), F32)
    gla_o, s_prompt = _gla_prompt(gq, gk, la, gv, gr, gla_g, s0, nb, 256)
    diff_o = _attn_prompt(lams, dqb, dkb, vt, diff_g, nb, seq, lam_init)
    y_prompt = dense_tail(xp, gla_o, diff_o, 256, 1024).reshape(nb, seq, D_MODEL)

    xs = x_sample.reshape(ns, D_MODEL)
    sgq, sgk, sla, sgv, sgr, sdqb, sdk, sdv = _inproj(xs, g1, wm, wc, wa2p, ba, ns)
    sgla_o, s_sample = _gla_step(sgq, sgk, sla, sgv, sgr, gla_g, state_gla[0])
    sdiff_o = _attn_decode(page_table, lams, sdqb, sdk, sdv, diff_g, cache_k, cache_v, lam_init, 8)
    y_sample = dense_tail(xs, sgla_o.reshape(ns, GLA_V), sdiff_o.reshape(ns, DIFF_V), ns, ns).reshape(ns, 1, D_MODEL)

    k_prompt = dk.reshape(1, nb, seq, DIFF_HEADS, 2 * DIFF_HEAD_DIM)
    v_prompt = dv.reshape(1, nb, seq, DIFF_HEADS, DIFF_V_DIM)
    k_sample = sdk.reshape(1, ns, 1, DIFF_HEADS, 2 * DIFF_HEAD_DIM)
    v_sample = sdv.reshape(1, ns, 1, DIFF_HEADS, DIFF_V_DIM)
    return (y_prompt, y_sample, k_prompt, v_prompt, s_prompt[None], k_sample, v_sample, s_sample[None])
```

```python
import functools
import math

import jax
import jax.numpy as jnp
from jax import lax
from jax.experimental import pallas as pl
from jax.experimental.pallas import tpu as pltpu

F32 = jnp.float32
BF16 = jnp.bfloat16

D_MODEL = 1024
GLA_HEADS = 4
GLA_DK = 64
GLA_DV = 128
GLA_RANK = 16
GLA_TAU = 16.0
GLA_CHUNK = 16
DIFF_HEADS = 4
DIFF_HEAD_DIM = 64
DIFF_V_DIM = 128
GLA_QK = GLA_HEADS * GLA_DK
GLA_V = GLA_HEADS * GLA_DV
DIFF_QK = DIFF_HEADS * 2 * DIFF_HEAD_DIM
DIFF_V = DIFF_HEADS * DIFF_V_DIM
N_GROUPS = 4
EXPERTS_PER_GROUP = 4
N_EXPERTS = 16
EXPERT_FF = 256
EPS = 1e-6
LANES = 128
BF16_SUBLANES = 16
LOG2E = math.log2(math.e)
DECODE_SLOTS = 3
VT_ROWS = DIFF_V_DIM + BF16_SUBLANES
VMEM_LIMIT = 56 * 1024 * 1024

_C_GQ, _C_GK, _C_GV, _C_GR, _C_DQ, _C_DK, _C_DV, _C_END = 0, 256, 512, 1024, 1536, 2048, 2560, 3072


def _rms(x, g):
    return x * lax.rsqrt(jnp.mean(x * x, axis=-1, keepdims=True) + EPS) * g


def _silu(x):
    return x * (1.0 / (1.0 + jnp.exp(-x)))


def _log_sigmoid(x):
    return jnp.minimum(x, 0.0) - jnp.log1p(jnp.exp(-jnp.abs(x)))


def _dot(a, b):
    return jnp.dot(a, b, preferred_element_type=F32)


def _dot_nt(a, b):
    return lax.dot_general(a, b, (((1,), (1,)), ((), ())), preferred_element_type=F32)


def _split3(x):
    hi = x.astype(BF16)
    r = x - hi.astype(F32)
    mid = r.astype(BF16)
    lo = (r - mid.astype(F32)).astype(BF16)
    return hi, mid, lo


def _params(sem):
    return pltpu.CompilerParams(dimension_semantics=sem, vmem_limit_bytes=VMEM_LIMIT)


def _inproj_kernel(x_ref, g_ref, wm_ref, wc_ref, wa2_ref, ba_ref,
                   gq_ref, gk_ref, la_ref, gv_ref, gr_ref, dqb_ref, dk_ref, dv_ref, *prompt_refs):
    h = _rms(x_ref[...], g_ref[...]).astype(BF16)

    def seg(lo, hi):
        return _dot(h, wm_ref[:, lo:hi])

    gq_ref[...] = seg(_C_GQ, _C_GK) * (GLA_DK ** -0.5)
    gk_ref[...] = seg(_C_GK, _C_GV)
    gv_ref[...] = seg(_C_GV, _C_GR)
    gr_ref[...] = seg(_C_GR, _C_DQ)
    dqb_ref[...] = (seg(_C_DQ, _C_DK) * (DIFF_HEAD_DIM ** -0.5 * LOG2E)).astype(BF16)
    dk = seg(_C_DK, _C_DV)
    dv = seg(_C_DV, _C_END)
    tm = dk.shape[0]
    for hd in range(DIFF_HEADS):
        rows = pl.ds(hd, tm, stride=DIFF_HEADS)
        dk_ref[rows, :] = dk[:, hd * LANES:(hd + 1) * LANES]
        dv_ref[rows, :] = dv[:, hd * LANES:(hd + 1) * LANES]
    if prompt_refs:
        dkb_ref, vt_ref = prompt_refs
        dkb_ref[...] = dk.astype(BF16)
        ones_row = (lax.broadcasted_iota(jnp.int32, (VT_ROWS - DIFF_V_DIM, tm), 0) == 0).astype(F32)
        for hd in range(DIFF_HEADS):
            vt_ref[0, hd, :DIFF_V_DIM, :] = dv[:, hd * DIFF_V_DIM:(hd + 1) * DIFF_V_DIM].T.astype(BF16)
            vt_ref[0, hd, DIFF_V_DIM:, :] = ones_row.astype(BF16)
    code = _dot(h, wc_ref[...])
    pre = _dot(code.astype(BF16), wa2_ref[...]) + ba_ref[...]
    la_ref[...] = _log_sigmoid(pre) * (1.0 / GLA_TAU)


def _inproj(x, g1, wm, wc, wa2p, ba, tm, prompt_shape=None):
    t = x.shape[0]
    row = lambda w: pl.BlockSpec((tm, w), lambda i: (i, 0))
    full = lambda a: pl.BlockSpec(a.shape, lambda i: (0,) * a.ndim)
    outs = [(GLA_QK, F32), (GLA_QK, F32), (GLA_QK, F32), (GLA_V, F32), (GLA_V, F32), (DIFF_QK, BF16)]
    out_specs = [row(w) for w, _ in outs]
    out_shape = [jax.ShapeDtypeStruct((t, w), dt) for w, dt in outs]
    out_specs += [pl.BlockSpec((tm * DIFF_HEADS, LANES), lambda i: (i, 0))] * 2
    out_shape += [jax.ShapeDtypeStruct((t * DIFF_HEADS, LANES), F32)] * 2
    if prompt_shape is not None:
        nb, seq = prompt_shape
        nl = seq // tm
        out_specs += [row(DIFF_QK), pl.BlockSpec((1, DIFF_HEADS, VT_ROWS, tm), lambda i: (i // nl, 0, 0, i % nl))]
        out_shape += [jax.ShapeDtypeStruct((t, DIFF_QK), BF16),
                      jax.ShapeDtypeStruct((nb, DIFF_HEADS, VT_ROWS, seq), BF16)]
    return pl.pallas_call(
        _inproj_kernel,
        grid=(t // tm,),
        in_specs=[row(D_MODEL), full(g1), full(wm), full(wc), full(wa2p), full(ba)],
        out_specs=out_specs,
        out_shape=out_shape,
        compiler_params=_params(("parallel",)),
        name="inproj",
    )(x, g1, wm, wc, wa2p, ba)


def _dot_tn(a, b):
    return lax.dot_general(a, b, (((0,), (0,)), ((), ())), preferred_element_type=F32)


def _gla_kernel(gq_ref, gk_ref, la_ref, gv_ref, gr_ref, tri_ref, last_ref, csel_ref, ind_ref, g_ref, s0_ref,
                o_ref, sfin_ref,
                s_s, b_s, qi_s, kd_s, adec_s, o_s, p_s):
    l = pl.program_id(1)
    tl = gq_ref.shape[0]
    c16 = GLA_CHUNK

    @pl.when(l == 0)
    def _():
        for h in range(GLA_HEADS):
            s_s[h * GLA_DK:(h + 1) * GLA_DK, :] = s0_ref[0, h]

    hi, mid, lo = _split3(la_ref[...])

    def cs(m):
        return (_dot(m, lo) + _dot(m, mid)) + _dot(m, hi)

    b = cs(tri_ref[...])
    bl = cs(last_ref[...])
    b_s[...] = b
    qi_s[...] = gq_ref[...] * jnp.exp(b)
    kd_s[...] = (gk_ref[...] * jnp.exp(bl - b)).astype(BF16)
    csel = csel_ref[...]
    adec_s[...] = jnp.exp((_dot_tn(lo, csel) + _dot_tn(mid, csel)) + _dot_tn(hi, csel))

    rowi = lax.broadcasted_iota(jnp.int32, (c16, GLA_QK), 0)
    lane_c = lax.broadcasted_iota(jnp.int32, (GLA_QK, LANES), 1)

    def chunk(c, carry):
        r0 = pl.multiple_of(c * c16, c16)
        q_c = gq_ref[pl.ds(r0, c16), :]
        b_c = b_s[pl.ds(r0, c16), :]
        for j in range(c16):
            bj = b_s[pl.ds(r0 + j, 1), :]
            kj = gk_ref[pl.ds(r0 + j, 1), :]
            e = jnp.exp(jnp.where(rowi >= j, b_c - bj, -jnp.inf))
            p_s[j * c16:(j + 1) * c16, :] = (q_c * e * kj).astype(BF16)
        r = _dot(p_s[...], ind_ref[...])
        qi_c = qi_s[pl.ds(r0, c16), :].astype(BF16)
        kd_c = kd_s[pl.ds(r0, c16), :]
        v_c = gv_ref[pl.ds(r0, c16), :]
        vb_c = v_c.astype(BF16)
        a_col = jnp.sum(jnp.where(lane_c == c, adec_s[...], 0.0), axis=1, keepdims=True)
        o_parts = []
        for h in range(GLA_HEADS):
            dk = slice(h * GLA_DK, (h + 1) * GLA_DK)
            dv = slice(h * GLA_DV, (h + 1) * GLA_DV)
            s_h = s_s[dk, :]
            o_parts.append(_dot(qi_c[:, dk], s_h.astype(BF16)))
            s_s[dk, :] = a_col[dk] * s_h + _dot_tn(kd_c[:, dk], vb_c[:, dv])
        o = jnp.concatenate(o_parts, axis=1)
        for j in range(c16):
            o = o + r[j * c16:(j + 1) * c16, :] * gv_ref[pl.ds(r0 + j, 1), :]
        o_s[pl.ds(r0, c16), :] = o
        return carry

    lax.fori_loop(0, tl // c16, chunk, 0, unroll=8)

    o = o_s[...]
    for h in range(GLA_HEADS):
        sl = slice(h * GLA_DV, (h + 1) * GLA_DV)
        o_ref[:, sl] = (_rms(o[:, sl], g_ref[...]) * _silu(gr_ref[:, sl])).astype(o_ref.dtype)

    @pl.when(l == pl.num_programs(1) - 1)
    def _():
        for h in range(GLA_HEADS):
            sfin_ref[0, h] = s_s[h * GLA_DK:(h + 1) * GLA_DK, :]


def _gla_prompt(gq, gk, la, gv, gr, gla_g, s0, nb, tl):
    t = gq.shape[0]
    nl = t // nb // tl
    ti = jnp.arange(tl)
    same = (ti[:, None] // GLA_CHUNK) == (ti[None, :] // GLA_CHUNK)
    tri = (same & (ti[None, :] <= ti[:, None])).astype(BF16)
    last = same.astype(BF16)
    csel = ((ti[:, None] // GLA_CHUNK) == jnp.arange(LANES)[None, :]).astype(BF16)
    hq = jnp.arange(GLA_QK) // GLA_DK
    hv = jnp.arange(GLA_V) // GLA_DV
    bd = hq[:, None] == hv[None, :]
    row = lambda w: pl.BlockSpec((tl, w), lambda b, l: (b * nl + l, 0))
    full = lambda shp: pl.BlockSpec(shp, lambda b, l: (0,) * len(shp))
    st = pl.BlockSpec((1, GLA_HEADS, GLA_DK, GLA_DV), lambda b, l: (b, 0, 0, 0))
    return pl.pallas_call(
        _gla_kernel,
        grid=(nb, nl),
        in_specs=[row(GLA_QK), row(GLA_QK), row(GLA_QK), row(GLA_V), row(GLA_V),
                  full((tl, tl)), full((tl, tl)), full((tl, LANES)), full((GLA_QK, GLA_V)),
                  full((1, GLA_DV)), st],
        out_specs=[row(GLA_V), st],
        out_shape=[jax.ShapeDtypeStruct((t, GLA_V), BF16),
                   jax.ShapeDtypeStruct((nb, GLA_HEADS, GLA_DK, GLA_DV), F32)],
        scratch_shapes=[pltpu.VMEM((GLA_QK, GLA_DV), F32), pltpu.VMEM((tl, GLA_QK), F32),
                        pltpu.VMEM((tl, GLA_QK), F32), pltpu.VMEM((tl, GLA_QK), BF16),
                        pltpu.VMEM((GLA_QK, LANES), F32),
                        pltpu.VMEM((tl, GLA_V), F32), pltpu.VMEM((GLA_CHUNK * GLA_CHUNK, GLA_QK), BF16)],
        compiler_params=_params(("parallel", "arbitrary")),
        name="gla_prompt",
    )(gq, gk, la, gv, gr, tri, last, csel, bd.astype(BF16), gla_g, s0)


def _gla_step_kernel(q_ref, k_ref, la_ref, v_ref, gr_ref, g_ref, s_ref, o_ref, snew_ref):
    for i in range(q_ref.shape[0]):
        for h in range(GLA_HEADS):
            dk = slice(h * GLA_DK, (h + 1) * GLA_DK)
            dv = slice(h * GLA_DV, (h + 1) * GLA_DV)
            a = jnp.exp(la_ref[i, dk, :])
            s_new = a * s_ref[i, h] + k_ref[i, dk, :] * v_ref[i, :, dv]
            snew_ref[i, h] = s_new
            o = jnp.sum(q_ref[i, dk, :] * s_new, axis=0, keepdims=True)
            o_ref[i, :, dv] = _rms(o, g_ref[...]) * _silu(gr_ref[i, :, dv])


def _gla_step(gq, gk, la, gv, gr, gla_g, state, per_step=8):
    n = gq.shape[0]
    assert n % per_step == 0
    col = pl.BlockSpec((per_step, GLA_QK, 1), lambda b: (b, 0, 0))
    rowv = pl.BlockSpec((per_step, 1, GLA_V), lambda b: (b, 0, 0))
    st = pl.BlockSpec((per_step, GLA_HEADS, GLA_DK, GLA_DV), lambda b: (b, 0, 0, 0))
    return pl.pallas_call(
        _gla_step_kernel,
        grid=(n // per_step,),
        in_specs=[col, col, col, rowv, rowv, pl.BlockSpec((1, GLA_DV), lambda b: (0, 0)), st],
        out_specs=[rowv, st],
        out_shape=[jax.ShapeDtypeStruct((n, 1, GLA_V), F32),
                   jax.ShapeDtypeStruct((n, GLA_HEADS, GLA_DK, GLA_DV), F32)],
        compiler_params=_params(("parallel",)),
        name="gla_step",
    )(gq.reshape(n, GLA_QK, 1), gk.reshape(n, GLA_QK, 1), la.reshape(n, GLA_QK, 1),
      gv.reshape(n, 1, GLA_V), gr.reshape(n, 1, GLA_V), gla_g, state)


def _lambda(lq1_ref, lk1_ref, lq2_ref, lk2_ref, lam_init):
    e1 = jnp.exp(jnp.sum(lq1_ref[...] * lk1_ref[...], axis=-1, keepdims=True))
    e2 = jnp.exp(jnp.sum(lq2_ref[...] * lk2_ref[...], axis=-1, keepdims=True))
    return e1 - e2 + lam_init


def _softmax_step(s, v, m, l, acc):
    m_new = jnp.maximum(m, jnp.max(s, axis=-1, keepdims=True))
    alpha = jnp.exp2(m - m_new)
    p = jnp.exp2(s - m_new)
    l = alpha * l + jnp.sum(p, axis=-1, keepdims=True)
    acc = alpha * acc + _dot(p.astype(BF16), v)
    return m_new, l, acc


def _attn_kernel(lq1_ref, lk1_ref, lq2_ref, lk2_ref, q_ref, k_ref, vt_ref, g_ref, o_ref, s_s, a_s,
                 *, lam_init, tq, ks, unroll):
    seq = q_ref.shape[0]
    lane = lax.broadcasted_iota(jnp.int32, (tq, LANES), 1)
    rel = lax.broadcasted_iota(jnp.int32, (ks, tq), 0) - lax.broadcasted_iota(jnp.int32, (ks, tq), 1)
    lam = _lambda(lq1_ref, lk1_ref, lq2_ref, lk2_ref, lam_init)
    n_mask = max(1, tq // ks)

    def q_tile(i, _):
        q0 = pl.multiple_of(i * tq, tq)
        q = q_ref[pl.ds(q0, tq), :]
        zero = jnp.zeros_like(q)
        qs = (jnp.where(lane < DIFF_HEAD_DIM, q, zero), jnp.where(lane >= DIFF_HEAD_DIM, q, zero))
        a_s[...] = jnp.zeros_like(a_s)
        n_full = (q0 + 1) // ks

        def score(j, slot, lo):
            k = k_ref[pl.ds(pl.multiple_of(j * ks, ks), ks), :]
            for mp in range(2):
                s_s[slot, mp, :, lo:] = _dot_nt(k, qs[mp][lo:])

        def step(j, cur, ms, masked, lo=0, lo_next=0):
            if lo_next is not None:
                score(j + 1, 1 - cur, lo_next)
            k0 = pl.multiple_of(j * ks, ks)
            vt = vt_ref[0, 0, :, pl.ds(k0, ks)]
            out = []
            for mp in range(2):
                s = s_s[cur, mp, :, lo:]
                if masked:
                    s = jnp.where(rel[:, lo:] <= q0 - k0, s, -jnp.inf)
                m_old = ms[mp][:, lo:]
                m_new = jnp.maximum(m_old, jnp.max(s, axis=0, keepdims=True))
                p = jnp.exp2(s - m_new).astype(BF16)
                a_s[mp, :, lo:] = jnp.exp2(m_old - m_new) * a_s[mp, :, lo:] + _dot(vt, p)
                out.append(m_new if lo == 0 else jnp.concatenate([ms[mp][:, :lo], m_new], axis=1))
            return tuple(out)

        minf = jnp.full((1, tq), -jnp.inf, F32)
        score(0, 0, 0)

        def trip(jj, ms):
            for u in range(unroll):
                ms = step(unroll * jj + u, u % 2, ms, False)
            return ms

        ms = lax.fori_loop(0, n_full // unroll, trip, (minf, minf))
        for r in range(n_mask):
            lo_next = (r + 1) * ks if r + 1 < n_mask else None
            ms = step(n_full + r, r % 2, ms, True, r * ks, lo_next)
        a1, a2 = a_s[0], a_s[1]
        w = a1[:DIFF_V_DIM] / a1[DIFF_V_DIM:DIFF_V_DIM + 1] - lam * (a2[:DIFF_V_DIM] / a2[DIFF_V_DIM:DIFF_V_DIM + 1])
        o_ref[pl.ds(q0, tq), :] = (_rms(w.T, g_ref[...]) * (1.0 - lam_init)).astype(o_ref.dtype)
        return 0

    lax.fori_loop(0, seq // tq, q_tile, 0)


def _attn_prompt(lams, dqb, dkb, vt, diff_g, nb, seq, lam_init, tq=1024, ks=256, unroll=4):
    t = dqb.shape[0]
    assert unroll % 2 == 0 and tq % (unroll * ks) == 0 and seq % tq == 0
    lam_spec = pl.BlockSpec((1, DIFF_HEAD_DIM), lambda b, h: (0, 0))
    seqspec = pl.BlockSpec((seq, LANES), lambda b, h: (b, h))
    return pl.pallas_call(
        functools.partial(_attn_kernel, lam_init=lam_init, tq=tq, ks=ks, unroll=unroll),
        grid=(nb, DIFF_HEADS),
        in_specs=[lam_spec] * 4 + [seqspec, seqspec, pl.BlockSpec((1, 1, VT_ROWS, seq), lambda b, h: (b, h, 0, 0)),
                                  pl.BlockSpec((1, DIFF_V_DIM), lambda b, h: (0, 0))],
        out_specs=seqspec,
        out_shape=jax.ShapeDtypeStruct((t, DIFF_V), BF16),
        scratch_shapes=[pltpu.VMEM((2, 2, ks, tq), F32), pltpu.VMEM((2, VT_ROWS, tq), F32)],
        compiler_params=_params(("parallel", "parallel")),
        name="attn_prompt",
    )(*lams, dqb, dkb, vt, diff_g)


def _decode_kernel(pt_ref, lq1_ref, lk1_ref, lq2_ref, lk2_ref, q_ref, kn_ref, vn_ref, g_ref, ck_ref, cv_ref,
                   o_ref, kbuf, vbuf, sem, m_s, l_s, acc_s, *, lam_init, ppc, nch, nseq, page_rows):
    b = pl.program_id(0)
    c = pl.program_id(1)
    step = b * nch + c
    slot = lax.rem(step, DECODE_SLOTS)
    nrow = 2 * DIFF_HEADS

    def copies(n):
        bb, cc, sl = n // nch, lax.rem(n, nch), lax.rem(n, DECODE_SLOTS)
        out = []
        for p in range(ppc):
            page = pt_ref[bb, cc * ppc + p]
            dst = pl.ds(p * page_rows, page_rows)
            out.append(pltpu.make_async_copy(ck_ref.at[page], kbuf.at[sl, dst, :], sem.at[0, sl]))
            out.append(pltpu.make_async_copy(cv_ref.at[page], vbuf.at[sl, dst, :], sem.at[1, sl]))
        return out

    ahead = DECODE_SLOTS - 1

    @pl.when(step == 0)
    def _():
        for n in range(ahead):
            for cp in copies(jnp.int32(n)):
                cp.start()

    @pl.when(step + ahead < nseq * nch)
    def _():
        for cp in copies(step + ahead):
            cp.start()

    @pl.when(c == 0)
    def _():
        m_s[...] = jnp.full_like(m_s, -jnp.inf)
        l_s[...] = jnp.zeros_like(l_s)
        acc_s[...] = jnp.zeros_like(acc_s)

    for cp in copies(step):
        cp.wait()

    rowi = lax.broadcasted_iota(jnp.int32, (nrow, LANES), 0)
    lane = lax.broadcasted_iota(jnp.int32, (nrow, LANES), 1)
    q4 = q_ref[0].astype(F32)
    q8 = jnp.concatenate([q4, q4], axis=0)
    qm = jnp.where((rowi < DIFF_HEADS) == (lane < DIFF_HEAD_DIM), q8, 0.0)

    keys = ppc * page_rows
    s = _dot_nt(qm.astype(BF16), kbuf[slot].astype(BF16))
    col = lax.broadcasted_iota(jnp.int32, (nrow, keys), 1)
    rowk = lax.broadcasted_iota(jnp.int32, (nrow, keys), 0)
    s = jnp.where((col % DIFF_HEADS) == (rowk % DIFF_HEADS), s, -jnp.inf)
    m, l, acc = _softmax_step(s, vbuf[slot].astype(BF16), m_s[...], l_s[...], acc_s[...])
    m_s[...] = m
    l_s[...] = l
    acc_s[...] = acc

    @pl.when(c == nch - 1)
    def _():
        lam = _lambda(lq1_ref, lk1_ref, lq2_ref, lk2_ref, lam_init)
        kn = kn_ref[0].astype(BF16).astype(F32)
        vn = vn_ref[0].astype(BF16).astype(F32)
        sn = jnp.sum(qm * jnp.concatenate([kn, kn], axis=0), axis=-1, keepdims=True)
        m_new = jnp.maximum(m, sn)
        alpha = jnp.exp2(m - m_new)
        p = jnp.exp2(sn - m_new)
        lf = alpha * l + p
        w = (alpha * acc + p.astype(BF16).astype(F32) * jnp.concatenate([vn, vn], axis=0)) / lf
        out = w[:DIFF_HEADS] - lam * w[DIFF_HEADS:]
        o_ref[0] = _rms(out, g_ref[...]) * (1.0 - lam_init)


def _attn_decode(page_table, lams, dqb, dk, dv, diff_g, cache_k, cache_v, lam_init, ppc):
    n, n_pages = page_table.shape
    nch = n_pages // ppc
    n_pool, page = cache_k.shape[1], cache_k.shape[2]
    page_rows = page * DIFF_HEADS
    ck = cache_k.reshape(n_pool, page_rows, 2 * DIFF_HEAD_DIM)
    cv = cache_v.reshape(n_pool, page_rows, DIFF_V_DIM)
    lam_spec = pl.BlockSpec((1, DIFF_HEAD_DIM), lambda b, c, pt: (0, 0))
    rowspec = pl.BlockSpec((1, DIFF_HEADS, LANES), lambda b, c, pt: (b, 0, 0))
    anyspec = pl.BlockSpec(memory_space=pl.ANY)
    kern = functools.partial(_decode_kernel, lam_init=lam_init, ppc=ppc, nch=nch, nseq=n, page_rows=page_rows)
    nrow = 2 * DIFF_HEADS
    return pl.pallas_call(
        kern,
        grid_spec=pltpu.PrefetchScalarGridSpec(
            num_scalar_prefetch=1,
            grid=(n, nch),
            in_specs=[lam_spec] * 4 + [rowspec, rowspec, rowspec,
                                      pl.BlockSpec((1, DIFF_V_DIM), lambda b, c, pt: (0, 0)), anyspec, anyspec],
            out_specs=rowspec,
            scratch_shapes=[pltpu.VMEM((DECODE_SLOTS, ppc * page_rows, LANES), F32),
                            pltpu.VMEM((DECODE_SLOTS, ppc * page_rows, LANES), F32),
                            pltpu.SemaphoreType.DMA((2, DECODE_SLOTS)),
                            pltpu.VMEM((nrow, 1), F32), pltpu.VMEM((nrow, 1), F32),
                            pltpu.VMEM((nrow, DIFF_V_DIM), F32)]),
        out_shape=jax.ShapeDtypeStruct((n, DIFF_HEADS, LANES), F32),
        compiler_params=_params(("arbitrary", "arbitrary")),
        name="attn_decode",
    )(page_table, *lams, dqb.reshape(n, DIFF_HEADS, LANES), dk.reshape(n, DIFF_HEADS, LANES),
      dv.reshape(n, DIFF_HEADS, LANES), diff_g, ck, cv)


def _post_kernel(x_ref, go_ref, do_ref, wo_ref, g2_ref, wrh_ref, wrl_ref, br_ref, x1_ref, t_ref, gates_ref):
    x1 = x_ref[...] + _dot(go_ref[...].astype(BF16), wo_ref[:GLA_V, :]) + _dot(do_ref[...].astype(BF16), wo_ref[GLA_V:, :])
    x1_ref[...] = x1
    t = _rms(x1, g2_ref[...])
    t_ref[...] = t.astype(BF16)
    t_hi = t.astype(BF16)
    t_lo = (t - t_hi.astype(F32)).astype(BF16)
    w_hi = wrh_ref[...]
    w_lo = wrl_ref[...]
    logits = (_dot(t_lo, w_hi) + _dot(t_hi, w_lo)) + _dot(t_hi, w_hi) + br_ref[...]
    lane = lax.broadcasted_iota(jnp.int32, logits.shape, 1)
    neg = -jnp.inf
    is_group = (lane >= N_EXPERTS) & (lane < N_EXPERTS + N_GROUPS)
    lg = jnp.where(is_group, logits, neg)
    lg_max = jnp.max(lg, axis=-1, keepdims=True)
    g_lane = jnp.min(jnp.where(lg == lg_max, lane, LANES), axis=-1, keepdims=True)
    p_top = 1.0 / jnp.sum(jnp.exp(lg - lg_max), axis=-1, keepdims=True)
    in_g = (lane // EXPERTS_PER_GROUP) == (g_lane - N_EXPERTS)
    le = jnp.where(in_g, logits, neg)
    v1 = jnp.max(le, axis=-1, keepdims=True)
    i1 = jnp.min(jnp.where(le == v1, lane, LANES), axis=-1, keepdims=True)
    le2 = jnp.where(lane == i1, neg, le)
    v2 = jnp.max(le2, axis=-1, keepdims=True)
    i2 = jnp.min(jnp.where(le2 == v2, lane, LANES), axis=-1, keepdims=True)
    e2 = jnp.exp(v2 - v1)
    w1 = p_top / (1.0 + e2)
    w2 = p_top * e2 / (1.0 + e2)
    gates_ref[...] = jnp.where(lane == i1, w1, 0.0) + jnp.where(lane == i2, w2, 0.0)


def _post(x, gla_o, diff_o, wo, g2, wr_hi, wr_lo, br, tm):
    t = x.shape[0]
    row = lambda w: pl.BlockSpec((tm, w), lambda i: (i, 0))
    full = lambda a: pl.BlockSpec(a.shape, lambda i: (0,) * a.ndim)
    return pl.pallas_call(
        _post_kernel,
        grid=(t // tm,),
        in_specs=[row(D_MODEL), row(GLA_V), row(DIFF_V), full(wo), full(g2), full(wr_hi), full(wr_lo), full(br)],
        out_specs=[row(D_MODEL), row(D_MODEL), row(LANES)],
        out_shape=[jax.ShapeDtypeStruct((t, D_MODEL), F32), jax.ShapeDtypeStruct((t, D_MODEL), BF16),
                   jax.ShapeDtypeStruct((t, LANES), F32)],
        compiler_params=_params(("parallel",)),
        name="post",
    )(x, gla_o, diff_o, wo, g2, wr_hi, wr_lo, br)


def _moe_kernel(t_ref, gates_ref, x1_ref, wgu_ref, wd_ref, gf_ref, y_ref, acc_s, *, eps_per_step):
    e0 = pl.program_id(1)

    @pl.when(e0 == 0)
    def _():
        acc_s[...] = jnp.zeros_like(acc_s)

    t = t_ref[...]
    gates = gates_ref[...]
    ei = lax.broadcasted_iota(jnp.int32, gates.shape, 1)
    acc = acc_s[...]
    for k in range(eps_per_step):
        gu = _dot(t, wgu_ref[k])
        gate = jnp.sum(jnp.where(ei == e0 * eps_per_step + k, gates, 0.0), axis=-1, keepdims=True)
        hid = _silu(gu[:, :EXPERT_FF]) * gu[:, EXPERT_FF:] * gate
        acc = acc + _dot(hid.astype(BF16), wd_ref[k])
    acc_s[...] = acc

    @pl.when(e0 == pl.num_programs(1) - 1)
    def _():
        y_ref[...] = _rms(x1_ref[...] + acc_s[...], gf_ref[...])


def _moe(t, gates, x1, wgu, wd, gf, tm, eps_per_step=EXPERTS_PER_GROUP):
    n = t.shape[0]
    row = lambda w: pl.BlockSpec((tm, w), lambda i, e: (i, 0))
    return pl.pallas_call(
        functools.partial(_moe_kernel, eps_per_step=eps_per_step),
        grid=(n // tm, N_EXPERTS // eps_per_step),
        in_specs=[row(D_MODEL), row(LANES), row(D_MODEL),
                  pl.BlockSpec((eps_per_step, D_MODEL, 2 * EXPERT_FF), lambda i, e: (e, 0, 0)),
                  pl.BlockSpec((eps_per_step, EXPERT_FF, D_MODEL), lambda i, e: (e, 0, 0)),
                  pl.BlockSpec((1, D_MODEL), lambda i, e: (0, 0))],
        out_specs=row(D_MODEL),
        out_shape=jax.ShapeDtypeStruct((n, D_MODEL), F32),
        scratch_shapes=[pltpu.VMEM((tm, D_MODEL), F32)],
        compiler_params=_params(("parallel", "arbitrary")),
        name="moe",
    )(t, gates, x1, wgu, wd, gf)


def kernel(x_prompt, x_sample, cache_k, cache_v, state_gla, page_table, norm1_g, w_in, w_a2, b_a, gla_norm_g,
           lambda_q1, lambda_k1, lambda_q2, lambda_k2, diff_norm_g, w_out, norm2_g, w_router_g, b_router_g,
           w_router_e, b_router_e, w_gate, w_up, w_down, norm_f_g):
    nb, seq, _ = x_prompt.shape
    ns = x_sample.shape[0]
    depth = w_in.shape[0]
    assert depth == 1 and x_sample.shape[1] == 1
    lam_init = 0.8 - 0.6 * math.exp(-0.3 * 0)

    c_code = 2 * GLA_QK + 2 * GLA_V
    w = w_in[0]
    wm = jnp.concatenate([w[:, :c_code], w[:, c_code + GLA_RANK:]], axis=1).astype(BF16)
    wc = jnp.pad(w[:, c_code:c_code + GLA_RANK], ((0, 0), (0, LANES - GLA_RANK))).astype(BF16)
    wa2p = jnp.pad(w_a2[0], ((0, LANES - GLA_RANK), (0, 0))).astype(BF16)
    ba = b_a[0].reshape(1, GLA_QK)
    g1 = norm1_g[0].reshape(1, D_MODEL)
    g2 = norm2_g[0].reshape(1, D_MODEL)
    gf = norm_f_g.reshape(1, D_MODEL)
    gla_g = gla_norm_g[0].reshape(1, GLA_DV)
    diff_g = diff_norm_g[0].reshape(1, DIFF_V_DIM)
    lams = [a[0].reshape(1, DIFF_HEAD_DIM) for a in (lambda_q1, lambda_k1, lambda_q2, lambda_k2)]
    wo = w_out[0].astype(BF16)
    wr = jnp.concatenate([w_router_e[0].transpose(1, 0, 2).reshape(D_MODEL, N_EXPERTS), w_router_g[0]], axis=1)
    wr = jnp.pad(wr, ((0, 0), (0, LANES - N_GROUPS - N_EXPERTS)))
    wr_hi = wr.astype(BF16)
    wr_lo = (wr - wr_hi.astype(F32)).astype(BF16)
    br = jnp.pad(jnp.concatenate([b_router_e[0].reshape(N_EXPERTS), b_router_g[0]]),
                 (0, LANES - N_GROUPS - N_EXPERTS)).reshape(1, LANES)
    wgu = jnp.concatenate([w_gate[0], w_up[0]], axis=-1).astype(BF16)
    wd = w_down[0].astype(BF16)

    def dense_tail(x, gla_o, diff_o, tm_post, tm_moe):
        x1, t, gates = _post(x, gla_o, diff_o, wo, g2, wr_hi, wr_lo, br, tm_post)
        return _moe(t, gates, x1, wgu, wd, gf, tm_moe)

    xp = x_prompt.reshape(nb * seq, D_MODEL)
    gq, gk, la, gv, gr, dqb, dk, dv, dkb, vt = _inproj(xp, g1, wm, wc, wa2p, ba, 512, (nb, seq))
    s0 = jnp.zeros((nb, GLA_HEADS, GLA_DK, GLA_DV), F32)
    gla_o, s_prompt = _gla_prompt(gq, gk, la, gv, gr, gla_g, s0, nb, 256)
    diff_o = _attn_prompt(lams, dqb, dkb, vt, diff_g, nb, seq, lam_init)
    y_prompt = dense_tail(xp, gla_o, diff_o, 512, 1024).reshape(nb, seq, D_MODEL)

    xs = x_sample.reshape(ns, D_MODEL)
    sgq, sgk, sla, sgv, sgr, sdqb, sdk, sdv = _inproj(xs, g1, wm, wc, wa2p, ba, ns)
    sgla_o, s_sample = _gla_step(sgq, sgk, sla, sgv, sgr, gla_g, state_gla[0])
    sdiff_o = _attn_decode(page_table, lams, sdqb, sdk, sdv, diff_g, cache_k, cache_v, lam_init, 8)
    y_sample = dense_tail(xs, sgla_o.reshape(ns, GLA_V), sdiff_o.reshape(ns, DIFF_V), ns, ns).reshape(ns, 1, D_MODEL)

    k_prompt = dk.reshape(1, nb, seq, DIFF_HEADS, 2 * DIFF_HEAD_DIM)
    v_prompt = dv.reshape(1, nb, seq, DIFF_HEADS, DIFF_V_DIM)
    k_sample = sdk.reshape(1, ns, 1, DIFF_HEADS, 2 * DIFF_HEAD_DIM)
    v_sample = sdv.reshape(1, ns, 1, DIFF_HEADS, DIFF_V_DIM)
    return (y_prompt, y_sample, k_prompt, v_prompt, s_prompt[None], k_sample, v_sample, s_sample[None])
```

```python
import functools
import math

import jax
import jax.numpy as jnp
from jax import lax
from jax.experimental import pallas as pl
from jax.experimental.pallas import tpu as pltpu

F32 = jnp.float32
BF16 = jnp.bfloat16

D_MODEL = 1024
GLA_HEADS = 4
GLA_DK = 64
GLA_DV = 128
GLA_RANK = 16
GLA_TAU = 16.0
GLA_CHUNK = 16
DIFF_HEADS = 4
DIFF_HEAD_DIM = 64
DIFF_V_DIM = 128
GLA_QK = GLA_HEADS * GLA_DK
GLA_V = GLA_HEADS * GLA_DV
DIFF_QK = DIFF_HEADS * 2 * DIFF_HEAD_DIM
DIFF_V = DIFF_HEADS * DIFF_V_DIM
N_GROUPS = 4
EXPERTS_PER_GROUP = 4
N_EXPERTS = 16
EXPERT_FF = 256
EPS = 1e-6
LANES = 128
BF16_SUBLANES = 16
LOG2E = math.log2(math.e)
MOE_BLOCK = 128
DECODE_SLOTS = 3
VT_ROWS = DIFF_V_DIM + BF16_SUBLANES
VMEM_LIMIT = 56 * 1024 * 1024

_C_GQ, _C_GK, _C_GV, _C_GR, _C_DQ, _C_DK, _C_DV, _C_END = 0, 256, 512, 1024, 1536, 2048, 2560, 3072


def _rms(x, g):
    return x * lax.rsqrt(jnp.mean(x * x, axis=-1, keepdims=True) + EPS) * g


def _silu(x):
    return x * (1.0 / (1.0 + jnp.exp(-x)))


def _log_sigmoid(x):
    return jnp.minimum(x, 0.0) - jnp.log1p(jnp.exp(-jnp.abs(x)))


def _dot(a, b):
    return jnp.dot(a, b, preferred_element_type=F32)


def _dot_nt(a, b):
    return lax.dot_general(a, b, (((1,), (1,)), ((), ())), preferred_element_type=F32)


def _split3(x):
    hi = x.astype(BF16)
    r = x - hi.astype(F32)
    mid = r.astype(BF16)
    lo = (r - mid.astype(F32)).astype(BF16)
    return hi, mid, lo


def _params(sem):
    return pltpu.CompilerParams(dimension_semantics=sem, vmem_limit_bytes=VMEM_LIMIT)


def _inproj_kernel(x_ref, g_ref, wm_ref, wc_ref, wa2_ref, ba_ref,
                   gq_ref, gk_ref, la_ref, gv_ref, gr_ref, dqb_ref, dk_ref, dv_ref, *prompt_refs):
    h = _rms(x_ref[...], g_ref[...]).astype(BF16)

    def seg(lo, hi):
        return _dot(h, wm_ref[:, lo:hi])

    gq_ref[...] = seg(_C_GQ, _C_GK) * (GLA_DK ** -0.5)
    gk_ref[...] = seg(_C_GK, _C_GV)
    gv_ref[...] = seg(_C_GV, _C_GR)
    gr_ref[...] = seg(_C_GR, _C_DQ)
    dqb_ref[...] = (seg(_C_DQ, _C_DK) * (DIFF_HEAD_DIM ** -0.5 * LOG2E)).astype(BF16)
    dk = seg(_C_DK, _C_DV)
    dv = seg(_C_DV, _C_END)
    tm = dk.shape[0]
    for hd in range(DIFF_HEADS):
        rows = pl.ds(hd, tm, stride=DIFF_HEADS)
        dk_ref[rows, :] = dk[:, hd * LANES:(hd + 1) * LANES]
        dv_ref[rows, :] = dv[:, hd * LANES:(hd + 1) * LANES]
    if prompt_refs:
        dkb_ref, vt_ref = prompt_refs
        dkb_ref[...] = dk.astype(BF16)
        ones_row = (lax.broadcasted_iota(jnp.int32, (VT_ROWS - DIFF_V_DIM, tm), 0) == 0).astype(F32)
        for hd in range(DIFF_HEADS):
            vt_ref[0, hd, :DIFF_V_DIM, :] = dv[:, hd * DIFF_V_DIM:(hd + 1) * DIFF_V_DIM].T.astype(BF16)
            vt_ref[0, hd, DIFF_V_DIM:, :] = ones_row.astype(BF16)
    code = _dot(h, wc_ref[...])
    pre = _dot(code.astype(BF16), wa2_ref[...]) + ba_ref[...]
    la_ref[...] = _log_sigmoid(pre) * (1.0 / GLA_TAU)


def _inproj(x, g1, wm, wc, wa2p, ba, tm, prompt_shape=None):
    t = x.shape[0]
    row = lambda w: pl.BlockSpec((tm, w), lambda i: (i, 0))
    full = lambda a: pl.BlockSpec(a.shape, lambda i: (0,) * a.ndim)
    outs = [(GLA_QK, F32), (GLA_QK, F32), (GLA_QK, F32), (GLA_V, F32), (GLA_V, F32), (DIFF_QK, BF16)]
    out_specs = [row(w) for w, _ in outs]
    out_shape = [jax.ShapeDtypeStruct((t, w), dt) for w, dt in outs]
    out_specs += [pl.BlockSpec((tm * DIFF_HEADS, LANES), lambda i: (i, 0))] * 2
    out_shape += [jax.ShapeDtypeStruct((t * DIFF_HEADS, LANES), F32)] * 2
    if prompt_shape is not None:
        nb, seq = prompt_shape
        nl = seq // tm
        out_specs += [row(DIFF_QK), pl.BlockSpec((1, DIFF_HEADS, VT_ROWS, tm), lambda i: (i // nl, 0, 0, i % nl))]
        out_shape += [jax.ShapeDtypeStruct((t, DIFF_QK), BF16),
                      jax.ShapeDtypeStruct((nb, DIFF_HEADS, VT_ROWS, seq), BF16)]
    return pl.pallas_call(
        _inproj_kernel,
        grid=(t // tm,),
        in_specs=[row(D_MODEL), full(g1), full(wm), full(wc), full(wa2p), full(ba)],
        out_specs=out_specs,
        out_shape=out_shape,
        compiler_params=_params(("parallel",)),
        name="inproj",
    )(x, g1, wm, wc, wa2p, ba)


def _dot_tn(a, b):
    return lax.dot_general(a, b, (((0,), (0,)), ((), ())), preferred_element_type=F32)


def _gla_kernel(gq_ref, gk_ref, la_ref, gv_ref, gr_ref, tri_ref, last_ref, csel_ref, ind_ref, g_ref, s0_ref,
                o_ref, sfin_ref,
                s_s, b_s, qi_s, kd_s, adec_s, o_s, p_s):
    l = pl.program_id(1)
    tl = gq_ref.shape[0]
    c16 = GLA_CHUNK

    @pl.when(l == 0)
    def _():
        for h in range(GLA_HEADS):
            s_s[h * GLA_DK:(h + 1) * GLA_DK, :] = s0_ref[0, h]

    hi, mid, lo = _split3(la_ref[...])

    def cs(m):
        return (_dot(m, lo) + _dot(m, mid)) + _dot(m, hi)

    b = cs(tri_ref[...])
    bl = cs(last_ref[...])
    b_s[...] = b
    qi_s[...] = gq_ref[...] * jnp.exp(b)
    kd_s[...] = (gk_ref[...] * jnp.exp(bl - b)).astype(BF16)
    csel = csel_ref[...]
    adec_s[...] = jnp.exp((_dot_tn(lo, csel) + _dot_tn(mid, csel)) + _dot_tn(hi, csel))

    rowi = lax.broadcasted_iota(jnp.int32, (c16, GLA_QK), 0)
    lane_c = lax.broadcasted_iota(jnp.int32, (GLA_QK, LANES), 1)

    def chunk(c, carry):
        r0 = pl.multiple_of(c * c16, c16)
        q_c = gq_ref[pl.ds(r0, c16), :]
        b_c = b_s[pl.ds(r0, c16), :]
        for j in range(c16):
            bj = b_s[pl.ds(r0 + j, 1), :]
            kj = gk_ref[pl.ds(r0 + j, 1), :]
            e = jnp.exp(jnp.where(rowi >= j, b_c - bj, -jnp.inf))
            p_s[j * c16:(j + 1) * c16, :] = (q_c * e * kj).astype(BF16)
        r = _dot(p_s[...], ind_ref[...])
        qi_c = qi_s[pl.ds(r0, c16), :].astype(BF16)
        kd_c = kd_s[pl.ds(r0, c16), :]
        v_c = gv_ref[pl.ds(r0, c16), :]
        vb_c = v_c.astype(BF16)
        a_col = jnp.sum(jnp.where(lane_c == c, adec_s[...], 0.0), axis=1, keepdims=True)
        o_parts = []
        for h in range(GLA_HEADS):
            dk = slice(h * GLA_DK, (h + 1) * GLA_DK)
            dv = slice(h * GLA_DV, (h + 1) * GLA_DV)
            s_h = s_s[dk, :]
            o_parts.append(_dot(qi_c[:, dk], s_h.astype(BF16)))
            s_s[dk, :] = a_col[dk] * s_h + _dot_tn(kd_c[:, dk], vb_c[:, dv])
        o = jnp.concatenate(o_parts, axis=1)
        for j in range(c16):
            o = o + r[j * c16:(j + 1) * c16, :] * gv_ref[pl.ds(r0 + j, 1), :]
        o_s[pl.ds(r0, c16), :] = o
        return carry

    lax.fori_loop(0, tl // c16, chunk, 0, unroll=True)

    o = o_s[...]
    for h in range(GLA_HEADS):
        sl = slice(h * GLA_DV, (h + 1) * GLA_DV)
        o_ref[:, sl] = (_rms(o[:, sl], g_ref[...]) * _silu(gr_ref[:, sl])).astype(o_ref.dtype)

    @pl.when(l == pl.num_programs(1) - 1)
    def _():
        for h in range(GLA_HEADS):
            sfin_ref[0, h] = s_s[h * GLA_DK:(h + 1) * GLA_DK, :]


def _gla_prompt(gq, gk, la, gv, gr, gla_g, s0, nb, tl):
    t = gq.shape[0]
    nl = t // nb // tl
    ti = jnp.arange(tl)
    same = (ti[:, None] // GLA_CHUNK) == (ti[None, :] // GLA_CHUNK)
    tri = (same & (ti[None, :] <= ti[:, None])).astype(BF16)
    last = same.astype(BF16)
    csel = ((ti[:, None] // GLA_CHUNK) == jnp.arange(LANES)[None, :]).astype(BF16)
    hq = jnp.arange(GLA_QK) // GLA_DK
    hv = jnp.arange(GLA_V) // GLA_DV
    bd = hq[:, None] == hv[None, :]
    row = lambda w: pl.BlockSpec((tl, w), lambda b, l: (b * nl + l, 0))
    full = lambda shp: pl.BlockSpec(shp, lambda b, l: (0,) * len(shp))
    st = pl.BlockSpec((1, GLA_HEADS, GLA_DK, GLA_DV), lambda b, l: (b, 0, 0, 0))
    return pl.pallas_call(
        _gla_kernel,
        grid=(nb, nl),
        in_specs=[row(GLA_QK), row(GLA_QK), row(GLA_QK), row(GLA_V), row(GLA_V),
                  full((tl, tl)), full((tl, tl)), full((tl, LANES)), full((GLA_QK, GLA_V)),
                  full((1, GLA_DV)), st],
        out_specs=[row(GLA_V), st],
        out_shape=[jax.ShapeDtypeStruct((t, GLA_V), BF16),
                   jax.ShapeDtypeStruct((nb, GLA_HEADS, GLA_DK, GLA_DV), F32)],
        scratch_shapes=[pltpu.VMEM((GLA_QK, GLA_DV), F32), pltpu.VMEM((tl, GLA_QK), F32),
                        pltpu.VMEM((tl, GLA_QK), F32), pltpu.VMEM((tl, GLA_QK), BF16),
                        pltpu.VMEM((GLA_QK, LANES), F32),
                        pltpu.VMEM((tl, GLA_V), F32), pltpu.VMEM((GLA_CHUNK * GLA_CHUNK, GLA_QK), BF16)],
        compiler_params=_params(("parallel", "arbitrary")),
        name="gla_prompt",
    )(gq, gk, la, gv, gr, tri, last, csel, bd.astype(BF16), gla_g, s0)


def _gla_step_kernel(q_ref, k_ref, la_ref, v_ref, gr_ref, g_ref, s_ref, o_ref, snew_ref):
    for i in range(q_ref.shape[0]):
        for h in range(GLA_HEADS):
            dk = slice(h * GLA_DK, (h + 1) * GLA_DK)
            dv = slice(h * GLA_DV, (h + 1) * GLA_DV)
            a = jnp.exp(la_ref[i, dk, :])
            s_new = a * s_ref[i, h] + k_ref[i, dk, :] * v_ref[i, :, dv]
            snew_ref[i, h] = s_new
            o = jnp.sum(q_ref[i, dk, :] * s_new, axis=0, keepdims=True)
            o_ref[i, :, dv] = _rms(o, g_ref[...]) * _silu(gr_ref[i, :, dv])


def _gla_step(gq, gk, la, gv, gr, gla_g, state, per_step=8):
    n = gq.shape[0]
    assert n % per_step == 0
    col = pl.BlockSpec((per_step, GLA_QK, 1), lambda b: (b, 0, 0))
    rowv = pl.BlockSpec((per_step, 1, GLA_V), lambda b: (b, 0, 0))
    st = pl.BlockSpec((per_step, GLA_HEADS, GLA_DK, GLA_DV), lambda b: (b, 0, 0, 0))
    return pl.pallas_call(
        _gla_step_kernel,
        grid=(n // per_step,),
        in_specs=[col, col, col, rowv, rowv, pl.BlockSpec((1, GLA_DV), lambda b: (0, 0)), st],
        out_specs=[rowv, st],
        out_shape=[jax.ShapeDtypeStruct((n, 1, GLA_V), F32),
                   jax.ShapeDtypeStruct((n, GLA_HEADS, GLA_DK, GLA_DV), F32)],
        compiler_params=_params(("parallel",)),
        name="gla_step",
    )(gq.reshape(n, GLA_QK, 1), gk.reshape(n, GLA_QK, 1), la.reshape(n, GLA_QK, 1),
      gv.reshape(n, 1, GLA_V), gr.reshape(n, 1, GLA_V), gla_g, state)


def _lambda(lq1_ref, lk1_ref, lq2_ref, lk2_ref, lam_init):
    e1 = jnp.exp(jnp.sum(lq1_ref[...] * lk1_ref[...], axis=-1, keepdims=True))
    e2 = jnp.exp(jnp.sum(lq2_ref[...] * lk2_ref[...], axis=-1, keepdims=True))
    return e1 - e2 + lam_init


def _softmax_step(s, v, m, l, acc):
    m_new = jnp.maximum(m, jnp.max(s, axis=-1, keepdims=True))
    alpha = jnp.exp2(m - m_new)
    p = jnp.exp2(s - m_new)
    l = alpha * l + jnp.sum(p, axis=-1, keepdims=True)
    acc = alpha * acc + _dot(p.astype(BF16), v)
    return m_new, l, acc


def _attn_kernel(lq1_ref, lk1_ref, lq2_ref, lk2_ref, q_ref, k_ref, vt_ref, g_ref, o_ref, s_s, a_s,
                 *, lam_init, tq, ks, unroll):
    seq = q_ref.shape[0]
    lane = lax.broadcasted_iota(jnp.int32, (tq, LANES), 1)
    rel = lax.broadcasted_iota(jnp.int32, (ks, tq), 0) - lax.broadcasted_iota(jnp.int32, (ks, tq), 1)
    lam = _lambda(lq1_ref, lk1_ref, lq2_ref, lk2_ref, lam_init)
    n_mask = max(1, tq // ks)

    def q_tile(i, _):
        q0 = pl.multiple_of(i * tq, tq)
        q = q_ref[pl.ds(q0, tq), :]
        zero = jnp.zeros_like(q)
        qs = (jnp.where(lane < DIFF_HEAD_DIM, q, zero), jnp.where(lane >= DIFF_HEAD_DIM, q, zero))
        a_s[...] = jnp.zeros_like(a_s)
        n_full = (q0 + 1) // ks

        def score(j, slot, lo):
            k = k_ref[pl.ds(pl.multiple_of(j * ks, ks), ks), :]
            for mp in range(2):
                s_s[slot, mp, :, lo:] = _dot_nt(k, qs[mp][lo:])

        def step(j, cur, ms, masked, lo=0, lo_next=0):
            if lo_next is not None:
                score(j + 1, 1 - cur, lo_next)
            k0 = pl.multiple_of(j * ks, ks)
            vt = vt_ref[0, 0, :, pl.ds(k0, ks)]
            out = []
            for mp in range(2):
                s = s_s[cur, mp, :, lo:]
                if masked:
                    s = jnp.where(rel[:, lo:] <= q0 - k0, s, -jnp.inf)
                m_old = ms[mp][:, lo:]
                m_new = jnp.maximum(m_old, jnp.max(s, axis=0, keepdims=True))
                p = jnp.exp2(s - m_new).astype(BF16)
                a_s[mp, :, lo:] = jnp.exp2(m_old - m_new) * a_s[mp, :, lo:] + _dot(vt, p)
                out.append(m_new if lo == 0 else jnp.concatenate([ms[mp][:, :lo], m_new], axis=1))
            return tuple(out)

        minf = jnp.full((1, tq), -jnp.inf, F32)
        score(0, 0, 0)

        def trip(jj, ms):
            for u in range(unroll):
                ms = step(unroll * jj + u, u % 2, ms, False)
            return ms

        ms = lax.fori_loop(0, n_full // unroll, trip, (minf, minf))
        for r in range(n_mask):
            lo_next = (r + 1) * ks if r + 1 < n_mask else None
            ms = step(n_full + r, r % 2, ms, True, r * ks, lo_next)
        a1, a2 = a_s[0], a_s[1]
        w = a1[:DIFF_V_DIM] / a1[DIFF_V_DIM:DIFF_V_DIM + 1] - lam * (a2[:DIFF_V_DIM] / a2[DIFF_V_DIM:DIFF_V_DIM + 1])
        o_ref[pl.ds(q0, tq), :] = (_rms(w.T, g_ref[...]) * (1.0 - lam_init)).astype(o_ref.dtype)
        return 0

    lax.fori_loop(0, seq // tq, q_tile, 0)


def _attn_prompt(lams, dqb, dkb, vt, diff_g, nb, seq, lam_init, tq=1024, ks=256, unroll=4):
    t = dqb.shape[0]
    assert unroll % 2 == 0 and tq % (unroll * ks) == 0 and seq % tq == 0
    lam_spec = pl.BlockSpec((1, DIFF_HEAD_DIM), lambda b, h: (0, 0))
    seqspec = pl.BlockSpec((seq, LANES), lambda b, h: (b, h))
    return pl.pallas_call(
        functools.partial(_attn_kernel, lam_init=lam_init, tq=tq, ks=ks, unroll=unroll),
        grid=(nb, DIFF_HEADS),
        in_specs=[lam_spec] * 4 + [seqspec, seqspec, pl.BlockSpec((1, 1, VT_ROWS, seq), lambda b, h: (b, h, 0, 0)),
                                  pl.BlockSpec((1, DIFF_V_DIM), lambda b, h: (0, 0))],
        out_specs=seqspec,
        out_shape=jax.ShapeDtypeStruct((t, DIFF_V), BF16),
        scratch_shapes=[pltpu.VMEM((2, 2, ks, tq), F32), pltpu.VMEM((2, VT_ROWS, tq), F32)],
        compiler_params=_params(("parallel", "parallel")),
        name="attn_prompt",
    )(*lams, dqb, dkb, vt, diff_g)


def _decode_kernel(pt_ref, lq1_ref, lk1_ref, lq2_ref, lk2_ref, q_ref, kn_ref, vn_ref, g_ref, ck_ref, cv_ref,
                   o_ref, kbuf, vbuf, sem, m_s, l_s, acc_s, *, lam_init, ppc, nch, nseq, page_rows):
    b = pl.program_id(0)
    c = pl.program_id(1)
    step = b * nch + c
    slot = lax.rem(step, DECODE_SLOTS)
    nrow = 2 * DIFF_HEADS

    def copies(n):
        bb, cc, sl = n // nch, lax.rem(n, nch), lax.rem(n, DECODE_SLOTS)
        out = []
        for p in range(ppc):
            page = pt_ref[bb, cc * ppc + p]
            dst = pl.ds(p * page_rows, page_rows)
            out.append(pltpu.make_async_copy(ck_ref.at[page], kbuf.at[sl, dst, :], sem.at[0, sl]))
            out.append(pltpu.make_async_copy(cv_ref.at[page], vbuf.at[sl, dst, :], sem.at[1, sl]))
        return out

    ahead = DECODE_SLOTS - 1

    @pl.when(step == 0)
    def _():
        for n in range(ahead):
            for cp in copies(jnp.int32(n)):
                cp.start()

    @pl.when(step + ahead < nseq * nch)
    def _():
        for cp in copies(step + ahead):
            cp.start()

    @pl.when(c == 0)
    def _():
        m_s[...] = jnp.full_like(m_s, -jnp.inf)
        l_s[...] = jnp.zeros_like(l_s)
        acc_s[...] = jnp.zeros_like(acc_s)

    for cp in copies(step):
        cp.wait()

    rowi = lax.broadcasted_iota(jnp.int32, (nrow, LANES), 0)
    lane = lax.broadcasted_iota(jnp.int32, (nrow, LANES), 1)
    q4 = q_ref[0].astype(F32)
    q8 = jnp.concatenate([q4, q4], axis=0)
    qm = jnp.where((rowi < DIFF_HEADS) == (lane < DIFF_HEAD_DIM), q8, 0.0)

    keys = ppc * page_rows
    s = _dot_nt(qm.astype(BF16), kbuf[slot].astype(BF16))
    col = lax.broadcasted_iota(jnp.int32, (nrow, keys), 1)
    rowk = lax.broadcasted_iota(jnp.int32, (nrow, keys), 0)
    s = jnp.where((col % DIFF_HEADS) == (rowk % DIFF_HEADS), s, -jnp.inf)
    m, l, acc = _softmax_step(s, vbuf[slot].astype(BF16), m_s[...], l_s[...], acc_s[...])
    m_s[...] = m
    l_s[...] = l
    acc_s[...] = acc

    @pl.when(c == nch - 1)
    def _():
        lam = _lambda(lq1_ref, lk1_ref, lq2_ref, lk2_ref, lam_init)
        kn = kn_ref[0].astype(BF16).astype(F32)
        vn = vn_ref[0].astype(BF16).astype(F32)
        sn = jnp.sum(qm * jnp.concatenate([kn, kn], axis=0), axis=-1, keepdims=True)
        m_new = jnp.maximum(m, sn)
        alpha = jnp.exp2(m - m_new)
        p = jnp.exp2(sn - m_new)
        lf = alpha * l + p
        w = (alpha * acc + p.astype(BF16).astype(F32) * jnp.concatenate([vn, vn], axis=0)) / lf
        out = w[:DIFF_HEADS] - lam * w[DIFF_HEADS:]
        o_ref[0] = _rms(out, g_ref[...]) * (1.0 - lam_init)


def _attn_decode(page_table, lams, dqb, dk, dv, diff_g, cache_k, cache_v, lam_init, ppc):
    n, n_pages = page_table.shape
    nch = n_pages // ppc
    n_pool, page = cache_k.shape[1], cache_k.shape[2]
    page_rows = page * DIFF_HEADS
    ck = cache_k.reshape(n_pool, page_rows, 2 * DIFF_HEAD_DIM)
    cv = cache_v.reshape(n_pool, page_rows, DIFF_V_DIM)
    lam_spec = pl.BlockSpec((1, DIFF_HEAD_DIM), lambda b, c, pt: (0, 0))
    rowspec = pl.BlockSpec((1, DIFF_HEADS, LANES), lambda b, c, pt: (b, 0, 0))
    anyspec = pl.BlockSpec(memory_space=pl.ANY)
    kern = functools.partial(_decode_kernel, lam_init=lam_init, ppc=ppc, nch=nch, nseq=n, page_rows=page_rows)
    nrow = 2 * DIFF_HEADS
    return pl.pallas_call(
        kern,
        grid_spec=pltpu.PrefetchScalarGridSpec(
            num_scalar_prefetch=1,
            grid=(n, nch),
            in_specs=[lam_spec] * 4 + [rowspec, rowspec, rowspec,
                                      pl.BlockSpec((1, DIFF_V_DIM), lambda b, c, pt: (0, 0)), anyspec, anyspec],
            out_specs=rowspec,
            scratch_shapes=[pltpu.VMEM((DECODE_SLOTS, ppc * page_rows, LANES), F32),
                            pltpu.VMEM((DECODE_SLOTS, ppc * page_rows, LANES), F32),
                            pltpu.SemaphoreType.DMA((2, DECODE_SLOTS)),
                            pltpu.VMEM((nrow, 1), F32), pltpu.VMEM((nrow, 1), F32),
                            pltpu.VMEM((nrow, DIFF_V_DIM), F32)]),
        out_shape=jax.ShapeDtypeStruct((n, DIFF_HEADS, LANES), F32),
        compiler_params=_params(("arbitrary", "arbitrary")),
        name="attn_decode",
    )(page_table, *lams, dqb.reshape(n, DIFF_HEADS, LANES), dk.reshape(n, DIFF_HEADS, LANES),
      dv.reshape(n, DIFF_HEADS, LANES), diff_g, ck, cv)


def _post_kernel(x_ref, go_ref, do_ref, wo_ref, g2_ref, wrh_ref, wrl_ref, br_ref, x1_ref, t_ref, gates_ref):
    x1 = x_ref[...] + _dot(go_ref[...].astype(BF16), wo_ref[:GLA_V, :]) + _dot(do_ref[...].astype(BF16), wo_ref[GLA_V:, :])
    x1_ref[...] = x1
    t = _rms(x1, g2_ref[...])
    t_ref[...] = t.astype(BF16)
    t_hi = t.astype(BF16)
    t_lo = (t - t_hi.astype(F32)).astype(BF16)
    w_hi = wrh_ref[...]
    w_lo = wrl_ref[...]
    logits = (_dot(t_lo, w_hi) + _dot(t_hi, w_lo)) + _dot(t_hi, w_hi) + br_ref[...]
    lane = lax.broadcasted_iota(jnp.int32, logits.shape, 1)
    neg = -jnp.inf
    is_group = (lane >= N_EXPERTS) & (lane < N_EXPERTS + N_GROUPS)
    lg = jnp.where(is_group, logits, neg)
    lg_max = jnp.max(lg, axis=-1, keepdims=True)
    g_lane = jnp.min(jnp.where(lg == lg_max, lane, LANES), axis=-1, keepdims=True)
    p_top = 1.0 / jnp.sum(jnp.exp(lg - lg_max), axis=-1, keepdims=True)
    in_g = (lane // EXPERTS_PER_GROUP) == (g_lane - N_EXPERTS)
    le = jnp.where(in_g, logits, neg)
    v1 = jnp.max(le, axis=-1, keepdims=True)
    i1 = jnp.min(jnp.where(le == v1, lane, LANES), axis=-1, keepdims=True)
    le2 = jnp.where(lane == i1, neg, le)
    v2 = jnp.max(le2, axis=-1, keepdims=True)
    i2 = jnp.min(jnp.where(le2 == v2, lane, LANES), axis=-1, keepdims=True)
    e2 = jnp.exp(v2 - v1)
    w1 = p_top / (1.0 + e2)
    w2 = p_top * e2 / (1.0 + e2)
    gates_ref[...] = jnp.where(lane == i1, w1, 0.0) + jnp.where(lane == i2, w2, 0.0)


def _post(x, gla_o, diff_o, wo, g2, wr_hi, wr_lo, br, tm):
    t = x.shape[0]
    row = lambda w: pl.BlockSpec((tm, w), lambda i: (i, 0))
    full = lambda a: pl.BlockSpec(a.shape, lambda i: (0,) * a.ndim)
    return pl.pallas_call(
        _post_kernel,
        grid=(t // tm,),
        in_specs=[row(D_MODEL), row(GLA_V), row(DIFF_V), full(wo), full(g2), full(wr_hi), full(wr_lo), full(br)],
        out_specs=[row(D_MODEL), row(D_MODEL), row(LANES)],
        out_shape=[jax.ShapeDtypeStruct((t, D_MODEL), F32), jax.ShapeDtypeStruct((t, D_MODEL), BF16),
                   jax.ShapeDtypeStruct((t, LANES), F32)],
        compiler_params=_params(("parallel",)),
        name="post",
    )(x, gla_o, diff_o, wo, g2, wr_hi, wr_lo, br)


def _moe_cap(tm):
    need = tm + N_GROUPS * (BF16_SUBLANES - 1) + MOE_BLOCK - 1
    return -(-need // LANES) * LANES


def _moe_kernel(t_ref, gates_ref, x1_ref, lstrict_ref, gf_ref, wgu_hbm, wd_hbm, y_ref,
                wgu_s, wd_s, sem, xs_s, gs_s, acc_s):
    @pl.when(pl.program_id(0) == 0)
    def _():
        cw = pltpu.make_async_copy(wgu_hbm, wgu_s, sem.at[0])
        cd = pltpu.make_async_copy(wd_hbm, wd_s, sem.at[1])
        cw.start()
        cd.start()
        cw.wait()
        cd.wait()

    tm = t_ref.shape[0]
    cap = xs_s.shape[0]
    gates = gates_ref[...]
    lane = lax.broadcasted_iota(jnp.int32, (tm, LANES), 1)
    gmax = jnp.max(gates, axis=-1, keepdims=True)
    grp = jnp.min(jnp.where(gates == gmax, lane, LANES), axis=-1, keepdims=True) // EXPERTS_PER_GROUP
    onehot = lane == grp
    member = jnp.where(onehot, 1.0, 0.0)
    before = _dot(lstrict_ref[...], member.astype(BF16))
    tot = jnp.sum(member, axis=0, keepdims=True).astype(jnp.int32)
    totp = ((tot + (BF16_SUBLANES - 1)) // BF16_SUBLANES) * BF16_SUBLANES
    n_tok = [tot[0, c] for c in range(N_GROUPS)]
    start = [jnp.int32(0)]
    for c in range(1, N_GROUPS):
        start.append(start[-1] + totp[0, c - 1])
    lane_row = lax.broadcasted_iota(jnp.int32, (1, LANES), 1)
    start_row = jnp.zeros((1, LANES), F32)
    for c in range(1, N_GROUPS):
        start_row = jnp.where(lane_row == c, start[c].astype(F32), start_row)
    pos_f = jnp.sum(jnp.where(onehot, before + start_row, 0.0), axis=-1, keepdims=True)
    eye = lax.broadcasted_iota(jnp.int32, (tm, tm), 0) == lax.broadcasted_iota(jnp.int32, (tm, tm), 1)
    pos_row = jnp.sum(jnp.where(eye, pos_f, 0.0), axis=0, keepdims=True).astype(jnp.int32)
    pos_col = pos_f.astype(jnp.int32)
    to_sorted = jnp.where(lax.broadcasted_iota(jnp.int32, (cap, tm), 0) == pos_row, 1.0, 0.0).astype(BF16)
    to_token = jnp.where(lax.broadcasted_iota(jnp.int32, (tm, cap), 1) == pos_col, 1.0, 0.0).astype(BF16)

    xs_s[...] = _dot(to_sorted, t_ref[...]).astype(BF16)
    g_hi, g_mid, g_lo = _split3(gates)
    gs_s[...] = (_dot(to_sorted, g_lo) + _dot(to_sorted, g_mid)) + _dot(to_sorted, g_hi)
    acc_s[...] = jnp.zeros_like(acc_s)

    for c in range(N_GROUPS):
        def block(b, carry, c=c):
            r0 = pl.multiple_of(start[c] + b * MOE_BLOCK, BF16_SUBLANES)
            rows = pl.ds(r0, MOE_BLOCK)
            x = xs_s[rows, :]
            gs = gs_s[rows, :]
            acc = acc_s[rows, :]
            for k in range(EXPERTS_PER_GROUP):
                e = c * EXPERTS_PER_GROUP + k
                gu = _dot(x, wgu_s[e])
                hid = _silu(gu[:, :EXPERT_FF]) * gu[:, EXPERT_FF:] * gs[:, e:e + 1]
                acc = acc + _dot(hid.astype(BF16), wd_s[e])
            acc_s[rows, :] = acc
            return carry

        lax.fori_loop(0, (n_tok[c] + (MOE_BLOCK - 1)) // MOE_BLOCK, block, 0)

    acc = acc_s[...]
    a_hi = acc.astype(BF16)
    a_lo = (acc - a_hi.astype(F32)).astype(BF16)
    moe = _dot(to_token, a_lo) + _dot(to_token, a_hi)
    y_ref[...] = _rms(x1_ref[...] + moe, gf_ref[...])


def _moe(t, gates, x1, wgu, wd, gf, tm):
    n = t.shape[0]
    cap = _moe_cap(tm)
    ti = jnp.arange(tm)
    lstrict = (ti[None, :] < ti[:, None]).astype(BF16)
    row = lambda w: pl.BlockSpec((tm, w), lambda i: (i, 0))
    full = lambda a: pl.BlockSpec(a.shape, lambda i: (0,) * a.ndim)
    anyspec = pl.BlockSpec(memory_space=pl.ANY)
    return pl.pallas_call(
        _moe_kernel,
        grid=(n // tm,),
        in_specs=[row(D_MODEL), row(LANES), row(D_MODEL), full(lstrict), full(gf), anyspec, anyspec],
        out_specs=row(D_MODEL),
        out_shape=jax.ShapeDtypeStruct((n, D_MODEL), F32),
        scratch_shapes=[pltpu.VMEM(wgu.shape, BF16), pltpu.VMEM(wd.shape, BF16), pltpu.SemaphoreType.DMA((2,)),
                        pltpu.VMEM((cap, D_MODEL), BF16), pltpu.VMEM((cap, LANES), F32),
                        pltpu.VMEM((cap, D_MODEL), F32)],
        compiler_params=_params(("arbitrary",)),
        name="moe",
    )(t, gates, x1, lstrict, gf, wgu, wd)


def kernel(x_prompt, x_sample, cache_k, cache_v, state_gla, page_table, norm1_g, w_in, w_a2, b_a, gla_norm_g,
           lambda_q1, lambda_k1, lambda_q2, lambda_k2, diff_norm_g, w_out, norm2_g, w_router_g, b_router_g,
           w_router_e, b_router_e, w_gate, w_up, w_down, norm_f_g):
    nb, seq, _ = x_prompt.shape
    ns = x_sample.shape[0]
    depth = w_in.shape[0]
    assert depth == 1 and x_sample.shape[1] == 1
    lam_init = 0.8 - 0.6 * math.exp(-0.3 * 0)

    c_code = 2 * GLA_QK + 2 * GLA_V
    w = w_in[0]
    wm = jnp.concatenate([w[:, :c_code], w[:, c_code + GLA_RANK:]], axis=1).astype(BF16)
    wc = jnp.pad(w[:, c_code:c_code + GLA_RANK], ((0, 0), (0, LANES - GLA_RANK))).astype(BF16)
    wa2p = jnp.pad(w_a2[0], ((0, LANES - GLA_RANK), (0, 0))).astype(BF16)
    ba = b_a[0].reshape(1, GLA_QK)
    g1 = norm1_g[0].reshape(1, D_MODEL)
    g2 = norm2_g[0].reshape(1, D_MODEL)
    gf = norm_f_g.reshape(1, D_MODEL)
    gla_g = gla_norm_g[0].reshape(1, GLA_DV)
    diff_g = diff_norm_g[0].reshape(1, DIFF_V_DIM)
    lams = [a[0].reshape(1, DIFF_HEAD_DIM) for a in (lambda_q1, lambda_k1, lambda_q2, lambda_k2)]
    wo = w_out[0].astype(BF16)
    wr = jnp.concatenate([w_router_e[0].transpose(1, 0, 2).reshape(D_MODEL, N_EXPERTS), w_router_g[0]], axis=1)
    wr = jnp.pad(wr, ((0, 0), (0, LANES - N_GROUPS - N_EXPERTS)))
    wr_hi = wr.astype(BF16)
    wr_lo = (wr - wr_hi.astype(F32)).astype(BF16)
    br = jnp.pad(jnp.concatenate([b_router_e[0].reshape(N_EXPERTS), b_router_g[0]]),
                 (0, LANES - N_GROUPS - N_EXPERTS)).reshape(1, LANES)
    wgu = jnp.concatenate([w_gate[0], w_up[0]], axis=-1).astype(BF16)
    wd = w_down[0].astype(BF16)

    def dense_tail(x, gla_o, diff_o, tm_post, tm_moe):
        x1, t, gates = _post(x, gla_o, diff_o, wo, g2, wr_hi, wr_lo, br, tm_post)
        return _moe(t, gates, x1, wgu, wd, gf, tm_moe)

    xp = x_prompt.reshape(nb * seq, D_MODEL)
    gq, gk, la, gv, gr, dqb, dk, dv, dkb, vt = _inproj(xp, g1, wm, wc, wa2p, ba, 512, (nb, seq))
    s0 = jnp.zeros((nb, GLA_HEADS, GLA_DK, GLA_DV), F32)
    gla_o, s_prompt = _gla_prompt(gq, gk, la, gv, gr, gla_g, s0, nb, 256)
    diff_o = _attn_prompt(lams, dqb, dkb, vt, diff_g, nb, seq, lam_init)
    y_prompt = dense_tail(xp, gla_o, diff_o, 512, 512).reshape(nb, seq, D_MODEL)

    xs = x_sample.reshape(ns, D_MODEL)
    sgq, sgk, sla, sgv, sgr, sdqb, sdk, sdv = _inproj(xs, g1, wm, wc, wa2p, ba, ns)
    sgla_o, s_sample = _gla_step(sgq, sgk, sla, sgv, sgr, gla_g, state_gla[0])
    sdiff_o = _attn_decode(page_table, lams, sdqb, sdk, sdv, diff_g, cache_k, cache_v, lam_init, 8)
    y_sample = dense_tail(xs, sgla_o.reshape(ns, GLA_V), sdiff_o.reshape(ns, DIFF_V), ns, ns).reshape(ns, 1, D_MODEL)

    k_prompt = dk.reshape(1, nb, seq, DIFF_HEADS, 2 * DIFF_HEAD_DIM)
    v_prompt = dv.reshape(1, nb, seq, DIFF_HEADS, DIFF_V_DIM)
    k_sample = sdk.reshape(1, ns, 1, DIFF_HEADS, 2 * DIFF_HEAD_DIM)
    v_sample = sdv.reshape(1, ns, 1, DIFF_HEADS, DIFF_V_DIM)
    return (y_prompt, y_sample, k_prompt, v_prompt, s_prompt[None], k_sample, v_sample, s_sample[None])
```

```python
import functools
import math

import jax
import jax.numpy as jnp
from jax import lax
from jax.experimental import pallas as pl
from jax.experimental.pallas import tpu as pltpu

F32 = jnp.float32
BF16 = jnp.bfloat16

D_MODEL = 1024
GLA_HEADS = 4
GLA_DK = 64
GLA_DV = 128
GLA_RANK = 16
GLA_TAU = 16.0
GLA_CHUNK = 16
DIFF_HEADS = 4
DIFF_HEAD_DIM = 64
DIFF_V_DIM = 128
GLA_QK = GLA_HEADS * GLA_DK
GLA_V = GLA_HEADS * GLA_DV
DIFF_QK = DIFF_HEADS * 2 * DIFF_HEAD_DIM
DIFF_V = DIFF_HEADS * DIFF_V_DIM
N_GROUPS = 4
EXPERTS_PER_GROUP = 4
N_EXPERTS = 16
EXPERT_FF = 256
EPS = 1e-6
LANES = 128
BF16_SUBLANES = 16
LOG2E = math.log2(math.e)
DECODE_SLOTS = 3
VT_ROWS = DIFF_V_DIM + BF16_SUBLANES
VMEM_LIMIT = 56 * 1024 * 1024

_C_GQ, _C_GK, _C_GV, _C_GR, _C_DQ, _C_DK, _C_DV, _C_END = 0, 256, 512, 1024, 1536, 2048, 2560, 3072


def _rms(x, g):
    return x * lax.rsqrt(jnp.mean(x * x, axis=-1, keepdims=True) + EPS) * g


def _silu(x):
    return x * (1.0 / (1.0 + jnp.exp(-x)))


def _log_sigmoid(x):
    return jnp.minimum(x, 0.0) - jnp.log1p(jnp.exp(-jnp.abs(x)))


def _dot(a, b):
    return jnp.dot(a, b, preferred_element_type=F32)


def _dot_nt(a, b):
    return lax.dot_general(a, b, (((1,), (1,)), ((), ())), preferred_element_type=F32)


def _split3(x):
    hi = x.astype(BF16)
    r = x - hi.astype(F32)
    mid = r.astype(BF16)
    lo = (r - mid.astype(F32)).astype(BF16)
    return hi, mid, lo


def _params(sem):
    return pltpu.CompilerParams(dimension_semantics=sem, vmem_limit_bytes=VMEM_LIMIT)


def _inproj_kernel(x_ref, g_ref, wm_ref, wc_ref, wa2_ref, ba_ref,
                   gq_ref, gk_ref, la_ref, gv_ref, gr_ref, dqb_ref, dk_ref, dv_ref, *prompt_refs):
    h = _rms(x_ref[...], g_ref[...]).astype(BF16)

    def seg(lo, hi):
        return _dot(h, wm_ref[:, lo:hi])

    gq_ref[...] = seg(_C_GQ, _C_GK) * (GLA_DK ** -0.5)
    gk_ref[...] = seg(_C_GK, _C_GV)
    gv_ref[...] = seg(_C_GV, _C_GR)
    gr_ref[...] = seg(_C_GR, _C_DQ)
    dqb_ref[...] = (seg(_C_DQ, _C_DK) * (DIFF_HEAD_DIM ** -0.5 * LOG2E)).astype(BF16)
    dk = seg(_C_DK, _C_DV)
    dv = seg(_C_DV, _C_END)
    tm = dk.shape[0]
    for hd in range(DIFF_HEADS):
        rows = pl.ds(hd, tm, stride=DIFF_HEADS)
        dk_ref[rows, :] = dk[:, hd * LANES:(hd + 1) * LANES]
        dv_ref[rows, :] = dv[:, hd * LANES:(hd + 1) * LANES]
    if prompt_refs:
        dkb_ref, vt_ref = prompt_refs
        dkb_ref[...] = dk.astype(BF16)
        ones_row = (lax.broadcasted_iota(jnp.int32, (VT_ROWS - DIFF_V_DIM, tm), 0) == 0).astype(F32)
        for hd in range(DIFF_HEADS):
            vt_ref[0, hd, :DIFF_V_DIM, :] = dv[:, hd * DIFF_V_DIM:(hd + 1) * DIFF_V_DIM].T.astype(BF16)
            vt_ref[0, hd, DIFF_V_DIM:, :] = ones_row.astype(BF16)
    code = _dot(h, wc_ref[...])
    pre = _dot(code.astype(BF16), wa2_ref[...]) + ba_ref[...]
    la_ref[...] = _log_sigmoid(pre) * (1.0 / GLA_TAU)


def _inproj(x, g1, wm, wc, wa2p, ba, tm, prompt_shape=None):
    t = x.shape[0]
    row = lambda w: pl.BlockSpec((tm, w), lambda i: (i, 0))
    full = lambda a: pl.BlockSpec(a.shape, lambda i: (0,) * a.ndim)
    outs = [(GLA_QK, F32), (GLA_QK, F32), (GLA_QK, F32), (GLA_V, F32), (GLA_V, F32), (DIFF_QK, BF16)]
    out_specs = [row(w) for w, _ in outs]
    out_shape = [jax.ShapeDtypeStruct((t, w), dt) for w, dt in outs]
    out_specs += [pl.BlockSpec((tm * DIFF_HEADS, LANES), lambda i: (i, 0))] * 2
    out_shape += [jax.ShapeDtypeStruct((t * DIFF_HEADS, LANES), F32)] * 2
    if prompt_shape is not None:
        nb, seq = prompt_shape
        nl = seq // tm
        out_specs += [row(DIFF_QK), pl.BlockSpec((1, DIFF_HEADS, VT_ROWS, tm), lambda i: (i // nl, 0, 0, i % nl))]
        out_shape += [jax.ShapeDtypeStruct((t, DIFF_QK), BF16),
                      jax.ShapeDtypeStruct((nb, DIFF_HEADS, VT_ROWS, seq), BF16)]
    return pl.pallas_call(
        _inproj_kernel,
        grid=(t // tm,),
        in_specs=[row(D_MODEL), full(g1), full(wm), full(wc), full(wa2p), full(ba)],
        out_specs=out_specs,
        out_shape=out_shape,
        compiler_params=_params(("parallel",)),
        name="inproj",
    )(x, g1, wm, wc, wa2p, ba)


def _dot_tn(a, b):
    return lax.dot_general(a, b, (((0,), (0,)), ((), ())), preferred_element_type=F32)


def _gla_kernel(gq_ref, gk_ref, la_ref, gv_ref, gr_ref, tri_ref, last_ref, csel_ref, ind_ref, g_ref, s0_ref,
                o_ref, sfin_ref,
                s_s, b_s, qi_s, kd_s, adec_s, o_s, p_s):
    l = pl.program_id(1)
    tl = gq_ref.shape[0]
    c16 = GLA_CHUNK

    @pl.when(l == 0)
    def _():
        for h in range(GLA_HEADS):
            s_s[h * GLA_DK:(h + 1) * GLA_DK, :] = s0_ref[0, h]

    hi, mid, lo = _split3(la_ref[...])

    def cs(m):
        return (_dot(m, lo) + _dot(m, mid)) + _dot(m, hi)

    b = cs(tri_ref[...])
    bl = cs(last_ref[...])
    b_s[...] = b
    qi_s[...] = gq_ref[...] * jnp.exp(b)
    kd_s[...] = (gk_ref[...] * jnp.exp(bl - b)).astype(BF16)
    csel = csel_ref[...]
    adec_s[...] = jnp.exp((_dot_tn(lo, csel) + _dot_tn(mid, csel)) + _dot_tn(hi, csel))

    rowi = lax.broadcasted_iota(jnp.int32, (c16, GLA_QK), 0)
    lane_c = lax.broadcasted_iota(jnp.int32, (GLA_QK, LANES), 1)

    def chunk(c, carry):
        r0 = pl.multiple_of(c * c16, c16)
        q_c = gq_ref[pl.ds(r0, c16), :]
        b_c = b_s[pl.ds(r0, c16), :]
        for j in range(c16):
            bj = b_s[pl.ds(r0 + j, 1), :]
            kj = gk_ref[pl.ds(r0 + j, 1), :]
            e = jnp.exp(jnp.where(rowi >= j, b_c - bj, -jnp.inf))
            p_s[j * c16:(j + 1) * c16, :] = (q_c * e * kj).astype(BF16)
        r = _dot(p_s[...], ind_ref[...])
        qi_c = qi_s[pl.ds(r0, c16), :].astype(BF16)
        kd_c = kd_s[pl.ds(r0, c16), :]
        v_c = gv_ref[pl.ds(r0, c16), :]
        vb_c = v_c.astype(BF16)
        a_col = jnp.sum(jnp.where(lane_c == c, adec_s[...], 0.0), axis=1, keepdims=True)
        o_parts = []
        for h in range(GLA_HEADS):
            dk = slice(h * GLA_DK, (h + 1) * GLA_DK)
            dv = slice(h * GLA_DV, (h + 1) * GLA_DV)
            s_h = s_s[dk, :]
            o_parts.append(_dot(qi_c[:, dk], s_h.astype(BF16)))
            s_s[dk, :] = a_col[dk] * s_h + _dot_tn(kd_c[:, dk], vb_c[:, dv])
        o = jnp.concatenate(o_parts, axis=1)
        for j in range(c16):
            o = o + r[j * c16:(j + 1) * c16, :] * gv_ref[pl.ds(r0 + j, 1), :]
        o_s[pl.ds(r0, c16), :] = o
        return carry

    lax.fori_loop(0, tl // c16, chunk, 0, unroll=True)

    o = o_s[...]
    for h in range(GLA_HEADS):
        sl = slice(h * GLA_DV, (h + 1) * GLA_DV)
        o_ref[:, sl] = (_rms(o[:, sl], g_ref[...]) * _silu(gr_ref[:, sl])).astype(o_ref.dtype)

    @pl.when(l == pl.num_programs(1) - 1)
    def _():
        for h in range(GLA_HEADS):
            sfin_ref[0, h] = s_s[h * GLA_DK:(h + 1) * GLA_DK, :]


def _gla_prompt(gq, gk, la, gv, gr, gla_g, s0, nb, tl):
    t = gq.shape[0]
    nl = t // nb // tl
    ti = jnp.arange(tl)
    same = (ti[:, None] // GLA_CHUNK) == (ti[None, :] // GLA_CHUNK)
    tri = (same & (ti[None, :] <= ti[:, None])).astype(BF16)
    last = same.astype(BF16)
    csel = ((ti[:, None] // GLA_CHUNK) == jnp.arange(LANES)[None, :]).astype(BF16)
    hq = jnp.arange(GLA_QK) // GLA_DK
    hv = jnp.arange(GLA_V) // GLA_DV
    bd = hq[:, None] == hv[None, :]
    row = lambda w: pl.BlockSpec((tl, w), lambda b, l: (b * nl + l, 0))
    full = lambda shp: pl.BlockSpec(shp, lambda b, l: (0,) * len(shp))
    st = pl.BlockSpec((1, GLA_HEADS, GLA_DK, GLA_DV), lambda b, l: (b, 0, 0, 0))
    return pl.pallas_call(
        _gla_kernel,
        grid=(nb, nl),
        in_specs=[row(GLA_QK), row(GLA_QK), row(GLA_QK), row(GLA_V), row(GLA_V),
                  full((tl, tl)), full((tl, tl)), full((tl, LANES)), full((GLA_QK, GLA_V)),
                  full((1, GLA_DV)), st],
        out_specs=[row(GLA_V), st],
        out_shape=[jax.ShapeDtypeStruct((t, GLA_V), BF16),
                   jax.ShapeDtypeStruct((nb, GLA_HEADS, GLA_DK, GLA_DV), F32)],
        scratch_shapes=[pltpu.VMEM((GLA_QK, GLA_DV), F32), pltpu.VMEM((tl, GLA_QK), F32),
                        pltpu.VMEM((tl, GLA_QK), F32), pltpu.VMEM((tl, GLA_QK), BF16),
                        pltpu.VMEM((GLA_QK, LANES), F32),
                        pltpu.VMEM((tl, GLA_V), F32), pltpu.VMEM((GLA_CHUNK * GLA_CHUNK, GLA_QK), BF16)],
        compiler_params=_params(("parallel", "arbitrary")),
        name="gla_prompt",
    )(gq, gk, la, gv, gr, tri, last, csel, bd.astype(BF16), gla_g, s0)


def _gla_step_kernel(q_ref, k_ref, la_ref, v_ref, gr_ref, g_ref, s_ref, o_ref, snew_ref):
    eye = (lax.broadcasted_iota(jnp.int32, (GLA_QK, GLA_QK), 0)
           == lax.broadcasted_iota(jnp.int32, (GLA_QK, GLA_QK), 1))

    def column(row):
        return jnp.sum(jnp.where(eye, row, 0.0), axis=1, keepdims=True)

    for i in range(q_ref.shape[0]):
        q_col, k_col, a_col = column(q_ref[i]), column(k_ref[i]), jnp.exp(column(la_ref[i]))
        for h in range(GLA_HEADS):
            dk = slice(h * GLA_DK, (h + 1) * GLA_DK)
            dv = slice(h * GLA_DV, (h + 1) * GLA_DV)
            s_new = a_col[dk] * s_ref[i, h] + k_col[dk] * v_ref[i, :, dv]
            snew_ref[i, h] = s_new
            o = jnp.sum(q_col[dk] * s_new, axis=0, keepdims=True)
            o_ref[i, :, dv] = _rms(o, g_ref[...]) * _silu(gr_ref[i, :, dv])


def _gla_step(gq, gk, la, gv, gr, gla_g, state, per_step=8):
    n = gq.shape[0]
    assert n % per_step == 0
    col = pl.BlockSpec((per_step, 1, GLA_QK), lambda b: (b, 0, 0))
    rowv = pl.BlockSpec((per_step, 1, GLA_V), lambda b: (b, 0, 0))
    st = pl.BlockSpec((per_step, GLA_HEADS, GLA_DK, GLA_DV), lambda b: (b, 0, 0, 0))
    return pl.pallas_call(
        _gla_step_kernel,
        grid=(n // per_step,),
        in_specs=[col, col, col, rowv, rowv, pl.BlockSpec((1, GLA_DV), lambda b: (0, 0)), st],
        out_specs=[rowv, st],
        out_shape=[jax.ShapeDtypeStruct((n, 1, GLA_V), F32),
                   jax.ShapeDtypeStruct((n, GLA_HEADS, GLA_DK, GLA_DV), F32)],
        compiler_params=_params(("parallel",)),
        name="gla_step",
    )(gq.reshape(n, 1, GLA_QK), gk.reshape(n, 1, GLA_QK), la.reshape(n, 1, GLA_QK),
      gv.reshape(n, 1, GLA_V), gr.reshape(n, 1, GLA_V), gla_g, state)


def _lambda(lq1_ref, lk1_ref, lq2_ref, lk2_ref, lam_init):
    e1 = jnp.exp(jnp.sum(lq1_ref[...] * lk1_ref[...], axis=-1, keepdims=True))
    e2 = jnp.exp(jnp.sum(lq2_ref[...] * lk2_ref[...], axis=-1, keepdims=True))
    return e1 - e2 + lam_init


def _softmax_step(s, v, m, l, acc):
    m_new = jnp.maximum(m, jnp.max(s, axis=-1, keepdims=True))
    alpha = jnp.exp2(m - m_new)
    p = jnp.exp2(s - m_new)
    l = alpha * l + jnp.sum(p, axis=-1, keepdims=True)
    acc = alpha * acc + _dot(p.astype(BF16), v)
    return m_new, l, acc


def _attn_kernel(lq1_ref, lk1_ref, lq2_ref, lk2_ref, q_ref, k_ref, vt_ref, g_ref, o_ref, s_s, a_s,
                 *, lam_init, tq, ks, unroll):
    seq = q_ref.shape[0]
    lane = lax.broadcasted_iota(jnp.int32, (tq, LANES), 1)
    rel = lax.broadcasted_iota(jnp.int32, (ks, tq), 0) - lax.broadcasted_iota(jnp.int32, (ks, tq), 1)
    lam = _lambda(lq1_ref, lk1_ref, lq2_ref, lk2_ref, lam_init)
    n_mask = max(1, tq // ks)

    def q_tile(i, _):
        q0 = pl.multiple_of(i * tq, tq)
        q = q_ref[pl.ds(q0, tq), :]
        zero = jnp.zeros_like(q)
        qs = (jnp.where(lane < DIFF_HEAD_DIM, q, zero), jnp.where(lane >= DIFF_HEAD_DIM, q, zero))
        a_s[...] = jnp.zeros_like(a_s)
        n_full = (q0 + 1) // ks

        def score(j, slot, lo):
            k = k_ref[pl.ds(pl.multiple_of(j * ks, ks), ks), :]
            for mp in range(2):
                s_s[slot, mp, :, lo:] = _dot_nt(k, qs[mp][lo:])

        def step(j, cur, ms, masked, lo=0, lo_next=0):
            if lo_next is not None:
                score(j + 1, 1 - cur, lo_next)
            k0 = pl.multiple_of(j * ks, ks)
            vt = vt_ref[0, 0, :, pl.ds(k0, ks)]
            out = []
            for mp in range(2):
                s = s_s[cur, mp, :, lo:]
                if masked:
                    s = jnp.where(rel[:, lo:] <= q0 - k0, s, -jnp.inf)
                m_old = ms[mp][:, lo:]
                m_new = jnp.maximum(m_old, jnp.max(s, axis=0, keepdims=True))
                p = jnp.exp2(s - m_new).astype(BF16)
                a_s[mp, :, lo:] = jnp.exp2(m_old - m_new) * a_s[mp, :, lo:] + _dot(vt, p)
                out.append(m_new if lo == 0 else jnp.concatenate([ms[mp][:, :lo], m_new], axis=1))
            return tuple(out)

        minf = jnp.full((1, tq), -jnp.inf, F32)
        score(0, 0, 0)

        def trip(jj, ms):
            for u in range(unroll):
                ms = step(unroll * jj + u, u % 2, ms, False)
            return ms

        ms = lax.fori_loop(0, n_full // unroll, trip, (minf, minf))
        for r in range(n_mask):
            lo_next = (r + 1) * ks if r + 1 < n_mask else None
            ms = step(n_full + r, r % 2, ms, True, r * ks, lo_next)
        a1, a2 = a_s[0], a_s[1]
        w = a1[:DIFF_V_DIM] / a1[DIFF_V_DIM:DIFF_V_DIM + 1] - lam * (a2[:DIFF_V_DIM] / a2[DIFF_V_DIM:DIFF_V_DIM + 1])
        o_ref[pl.ds(q0, tq), :] = (_rms(w.T, g_ref[...]) * (1.0 - lam_init)).astype(o_ref.dtype)
        return 0

    lax.fori_loop(0, seq // tq, q_tile, 0)


def _attn_prompt(lams, dqb, dkb, vt, diff_g, nb, seq, lam_init, tq=1024, ks=256, unroll=4):
    t = dqb.shape[0]
    assert unroll % 2 == 0 and tq % (unroll * ks) == 0 and seq % tq == 0
    lam_spec = pl.BlockSpec((1, DIFF_HEAD_DIM), lambda b, h: (0, 0))
    seqspec = pl.BlockSpec((seq, LANES), lambda b, h: (b, h))
    return pl.pallas_call(
        functools.partial(_attn_kernel, lam_init=lam_init, tq=tq, ks=ks, unroll=unroll),
        grid=(nb, DIFF_HEADS),
        in_specs=[lam_spec] * 4 + [seqspec, seqspec, pl.BlockSpec((1, 1, VT_ROWS, seq), lambda b, h: (b, h, 0, 0)),
                                  pl.BlockSpec((1, DIFF_V_DIM), lambda b, h: (0, 0))],
        out_specs=seqspec,
        out_shape=jax.ShapeDtypeStruct((t, DIFF_V), BF16),
        scratch_shapes=[pltpu.VMEM((2, 2, ks, tq), F32), pltpu.VMEM((2, VT_ROWS, tq), F32)],
        compiler_params=_params(("parallel", "parallel")),
        name="attn_prompt",
    )(*lams, dqb, dkb, vt, diff_g)


def _decode_kernel(pt_ref, lq1_ref, lk1_ref, lq2_ref, lk2_ref, q_ref, kn_ref, vn_ref, g_ref, ck_ref, cv_ref,
                   o_ref, kbuf, vbuf, sem, m_s, l_s, acc_s, *, lam_init, ppc, nch, nseq, page_rows):
    b = pl.program_id(0)
    c = pl.program_id(1)
    step = b * nch + c
    slot = lax.rem(step, DECODE_SLOTS)
    nrow = 2 * DIFF_HEADS

    def copies(n):
        bb, cc, sl = n // nch, lax.rem(n, nch), lax.rem(n, DECODE_SLOTS)
        out = []
        for p in range(ppc):
            page = pt_ref[bb, cc * ppc + p]
            dst = pl.ds(p * page_rows, page_rows)
            out.append(pltpu.make_async_copy(ck_ref.at[page], kbuf.at[sl, dst, :], sem.at[0, sl]))
            out.append(pltpu.make_async_copy(cv_ref.at[page], vbuf.at[sl, dst, :], sem.at[1, sl]))
        return out

    ahead = DECODE_SLOTS - 1

    @pl.when(step == 0)
    def _():
        for n in range(ahead):
            for cp in copies(jnp.int32(n)):
                cp.start()

    @pl.when(step + ahead < nseq * nch)
    def _():
        for cp in copies(step + ahead):
            cp.start()

    @pl.when(c == 0)
    def _():
        m_s[...] = jnp.full_like(m_s, -jnp.inf)
        l_s[...] = jnp.zeros_like(l_s)
        acc_s[...] = jnp.zeros_like(acc_s)

    for cp in copies(step):
        cp.wait()

    rowi = lax.broadcasted_iota(jnp.int32, (nrow, LANES), 0)
    lane = lax.broadcasted_iota(jnp.int32, (nrow, LANES), 1)
    q4 = q_ref[0].astype(F32)
    q8 = jnp.concatenate([q4, q4], axis=0)
    qm = jnp.where((rowi < DIFF_HEADS) == (lane < DIFF_HEAD_DIM), q8, 0.0)

    keys = ppc * page_rows
    s = _dot_nt(qm.astype(BF16), kbuf[slot].astype(BF16))
    col = lax.broadcasted_iota(jnp.int32, (nrow, keys), 1)
    rowk = lax.broadcasted_iota(jnp.int32, (nrow, keys), 0)
    s = jnp.where((col % DIFF_HEADS) == (rowk % DIFF_HEADS), s, -jnp.inf)
    m, l, acc = _softmax_step(s, vbuf[slot].astype(BF16), m_s[...], l_s[...], acc_s[...])
    m_s[...] = m
    l_s[...] = l
    acc_s[...] = acc

    @pl.when(c == nch - 1)
    def _():
        lam = _lambda(lq1_ref, lk1_ref, lq2_ref, lk2_ref, lam_init)
        kn = kn_ref[0].astype(BF16).astype(F32)
        vn = vn_ref[0].astype(BF16).astype(F32)
        sn = jnp.sum(qm * jnp.concatenate([kn, kn], axis=0), axis=-1, keepdims=True)
        m_new = jnp.maximum(m, sn)
        alpha = jnp.exp2(m - m_new)
        p = jnp.exp2(sn - m_new)
        lf = alpha * l + p
        w = (alpha * acc + p.astype(BF16).astype(F32) * jnp.concatenate([vn, vn], axis=0)) / lf
        out = w[:DIFF_HEADS] - lam * w[DIFF_HEADS:]
        o_ref[0] = _rms(out, g_ref[...]) * (1.0 - lam_init)


def _attn_decode(page_table, lams, dqb, dk, dv, diff_g, cache_k, cache_v, lam_init, ppc):
    n, n_pages = page_table.shape
    nch = n_pages // ppc
    n_pool, page = cache_k.shape[1], cache_k.shape[2]
    page_rows = page * DIFF_HEADS
    ck = cache_k.reshape(n_pool, page_rows, 2 * DIFF_HEAD_DIM)
    cv = cache_v.reshape(n_pool, page_rows, DIFF_V_DIM)
    lam_spec = pl.BlockSpec((1, DIFF_HEAD_DIM), lambda b, c, pt: (0, 0))
    rowspec = pl.BlockSpec((1, DIFF_HEADS, LANES), lambda b, c, pt: (b, 0, 0))
    anyspec = pl.BlockSpec(memory_space=pl.ANY)
    kern = functools.partial(_decode_kernel, lam_init=lam_init, ppc=ppc, nch=nch, nseq=n, page_rows=page_rows)
    nrow = 2 * DIFF_HEADS
    return pl.pallas_call(
        kern,
        grid_spec=pltpu.PrefetchScalarGridSpec(
            num_scalar_prefetch=1,
            grid=(n, nch),
            in_specs=[lam_spec] * 4 + [rowspec, rowspec, rowspec,
                                      pl.BlockSpec((1, DIFF_V_DIM), lambda b, c, pt: (0, 0)), anyspec, anyspec],
            out_specs=rowspec,
            scratch_shapes=[pltpu.VMEM((DECODE_SLOTS, ppc * page_rows, LANES), F32),
                            pltpu.VMEM((DECODE_SLOTS, ppc * page_rows, LANES), F32),
                            pltpu.SemaphoreType.DMA((2, DECODE_SLOTS)),
                            pltpu.VMEM((nrow, 1), F32), pltpu.VMEM((nrow, 1), F32),
                            pltpu.VMEM((nrow, DIFF_V_DIM), F32)]),
        out_shape=jax.ShapeDtypeStruct((n, DIFF_HEADS, LANES), F32),
        compiler_params=_params(("arbitrary", "arbitrary")),
        name="attn_decode",
    )(page_table, *lams, dqb.reshape(n, DIFF_HEADS, LANES), dk.reshape(n, DIFF_HEADS, LANES),
      dv.reshape(n, DIFF_HEADS, LANES), diff_g, ck, cv)


def _post_kernel(x_ref, go_ref, do_ref, wo_ref, g2_ref, wrh_ref, wrl_ref, br_ref, x1_ref, t_ref, gates_ref):
    x1 = x_ref[...] + _dot(go_ref[...].astype(BF16), wo_ref[:GLA_V, :]) + _dot(do_ref[...].astype(BF16), wo_ref[GLA_V:, :])
    x1_ref[...] = x1
    t = _rms(x1, g2_ref[...])
    t_ref[...] = t.astype(BF16)
    t_hi = t.astype(BF16)
    t_lo = (t - t_hi.astype(F32)).astype(BF16)
    w_hi = wrh_ref[...]
    w_lo = wrl_ref[...]
    logits = (_dot(t_lo, w_hi) + _dot(t_hi, w_lo)) + _dot(t_hi, w_hi) + br_ref[...]
    lane = lax.broadcasted_iota(jnp.int32, logits.shape, 1)
    neg = -jnp.inf
    is_group = (lane >= N_EXPERTS) & (lane < N_EXPERTS + N_GROUPS)
    lg = jnp.where(is_group, logits, neg)
    lg_max = jnp.max(lg, axis=-1, keepdims=True)
    g_lane = jnp.min(jnp.where(lg == lg_max, lane, LANES), axis=-1, keepdims=True)
    p_top = 1.0 / jnp.sum(jnp.exp(lg - lg_max), axis=-1, keepdims=True)
    in_g = (lane // EXPERTS_PER_GROUP) == (g_lane - N_EXPERTS)
    le = jnp.where(in_g, logits, neg)
    v1 = jnp.max(le, axis=-1, keepdims=True)
    i1 = jnp.min(jnp.where(le == v1, lane, LANES), axis=-1, keepdims=True)
    le2 = jnp.where(lane == i1, neg, le)
    v2 = jnp.max(le2, axis=-1, keepdims=True)
    i2 = jnp.min(jnp.where(le2 == v2, lane, LANES), axis=-1, keepdims=True)
    e2 = jnp.exp(v2 - v1)
    w1 = p_top / (1.0 + e2)
    w2 = p_top * e2 / (1.0 + e2)
    gates_ref[...] = jnp.where(lane == i1, w1, 0.0) + jnp.where(lane == i2, w2, 0.0)


def _post(x, gla_o, diff_o, wo, g2, wr_hi, wr_lo, br, tm):
    t = x.shape[0]
    row = lambda w: pl.BlockSpec((tm, w), lambda i: (i, 0))
    full = lambda a: pl.BlockSpec(a.shape, lambda i: (0,) * a.ndim)
    return pl.pallas_call(
        _post_kernel,
        grid=(t // tm,),
        in_specs=[row(D_MODEL), row(GLA_V), row(DIFF_V), full(wo), full(g2), full(wr_hi), full(wr_lo), full(br)],
        out_specs=[row(D_MODEL), row(D_MODEL), row(LANES)],
        out_shape=[jax.ShapeDtypeStruct((t, D_MODEL), F32), jax.ShapeDtypeStruct((t, D_MODEL), BF16),
                   jax.ShapeDtypeStruct((t, LANES), F32)],
        compiler_params=_params(("parallel",)),
        name="post",
    )(x, gla_o, diff_o, wo, g2, wr_hi, wr_lo, br)


def _moe_kernel(t_ref, gates_ref, x1_ref, wgu_ref, wd_ref, gf_ref, y_ref, acc_s, *, eps_per_step):
    e0 = pl.program_id(1)

    @pl.when(e0 == 0)
    def _():
        acc_s[...] = jnp.zeros_like(acc_s)

    t = t_ref[...]
    gates = gates_ref[...]
    ei = lax.broadcasted_iota(jnp.int32, gates.shape, 1)
    acc = acc_s[...]
    for k in range(eps_per_step):
        gu = _dot(t, wgu_ref[k])
        gate = jnp.sum(jnp.where(ei == e0 * eps_per_step + k, gates, 0.0), axis=-1, keepdims=True)
        hid = _silu(gu[:, :EXPERT_FF]) * gu[:, EXPERT_FF:] * gate
        acc = acc + _dot(hid.astype(BF16), wd_ref[k])
    acc_s[...] = acc

    @pl.when(e0 == pl.num_programs(1) - 1)
    def _():
        y_ref[...] = _rms(x1_ref[...] + acc_s[...], gf_ref[...])


def _moe(t, gates, x1, wgu, wd, gf, tm, eps_per_step=EXPERTS_PER_GROUP):
    n = t.shape[0]
    row = lambda w: pl.BlockSpec((tm, w), lambda i, e: (i, 0))
    return pl.pallas_call(
        functools.partial(_moe_kernel, eps_per_step=eps_per_step),
        grid=(n // tm, N_EXPERTS // eps_per_step),
        in_specs=[row(D_MODEL), row(LANES), row(D_MODEL),
                  pl.BlockSpec((eps_per_step, D_MODEL, 2 * EXPERT_FF), lambda i, e: (e, 0, 0)),
                  pl.BlockSpec((eps_per_step, EXPERT_FF, D_MODEL), lambda i, e: (e, 0, 0)),
                  pl.BlockSpec((1, D_MODEL), lambda i, e: (0, 0))],
        out_specs=row(D_MODEL),
        out_shape=jax.ShapeDtypeStruct((n, D_MODEL), F32),
        scratch_shapes=[pltpu.VMEM((tm, D_MODEL), F32)],
        compiler_params=_params(("parallel", "arbitrary")),
        name="moe",
    )(t, gates, x1, wgu, wd, gf)


def kernel(x_prompt, x_sample, cache_k, cache_v, state_gla, page_table, norm1_g, w_in, w_a2, b_a, gla_norm_g,
           lambda_q1, lambda_k1, lambda_q2, lambda_k2, diff_norm_g, w_out, norm2_g, w_router_g, b_router_g,
           w_router_e, b_router_e, w_gate, w_up, w_down, norm_f_g):
    nb, seq, _ = x_prompt.shape
    ns = x_sample.shape[0]
    depth = w_in.shape[0]
    assert depth == 1 and x_sample.shape[1] == 1
    lam_init = 0.8 - 0.6 * math.exp(-0.3 * 0)

    c_code = 2 * GLA_QK + 2 * GLA_V
    w = w_in[0]
    wm = jnp.concatenate([w[:, :c_code], w[:, c_code + GLA_RANK:]], axis=1).astype(BF16)
    wc = jnp.pad(w[:, c_code:c_code + GLA_RANK], ((0, 0), (0, LANES - GLA_RANK))).astype(BF16)
    wa2p = jnp.pad(w_a2[0], ((0, LANES - GLA_RANK), (0, 0))).astype(BF16)
    ba = b_a[0].reshape(1, GLA_QK)
    g1 = norm1_g[0].reshape(1, D_MODEL)
    g2 = norm2_g[0].reshape(1, D_MODEL)
    gf = norm_f_g.reshape(1, D_MODEL)
    gla_g = gla_norm_g[0].reshape(1, GLA_DV)
    diff_g = diff_norm_g[0].reshape(1, DIFF_V_DIM)
    lams = [a[0].reshape(1, DIFF_HEAD_DIM) for a in (lambda_q1, lambda_k1, lambda_q2, lambda_k2)]
    wo = w_out[0].astype(BF16)
    wr = jnp.concatenate([w_router_e[0].transpose(1, 0, 2).reshape(D_MODEL, N_EXPERTS), w_router_g[0]], axis=1)
    wr = jnp.pad(wr, ((0, 0), (0, LANES - N_GROUPS - N_EXPERTS)))
    wr_hi = wr.astype(BF16)
    wr_lo = (wr - wr_hi.astype(F32)).astype(BF16)
    br = jnp.pad(jnp.concatenate([b_router_e[0].reshape(N_EXPERTS), b_router_g[0]]),
                 (0, LANES - N_GROUPS - N_EXPERTS)).reshape(1, LANES)
    wgu = jnp.concatenate([w_gate[0], w_up[0]], axis=-1).astype(BF16)
    wd = w_down[0].astype(BF16)

    def dense_tail(x, gla_o, diff_o, tm_post, tm_moe):
        x1, t, gates = _post(x, gla_o, diff_o, wo, g2, wr_hi, wr_lo, br, tm_post)
        return _moe(t, gates, x1, wgu, wd, gf, tm_moe)

    xp = x_prompt.reshape(nb * seq, D_MODEL)
    gq, gk, la, gv, gr, dqb, dk, dv, dkb, vt = _inproj(xp, g1, wm, wc, wa2p, ba, 512, (nb, seq))
    s0 = jnp.zeros((nb, GLA_HEADS, GLA_DK, GLA_DV), F32)
    gla_o, s_prompt = _gla_prompt(gq, gk, la, gv, gr, gla_g, s0, nb, 256)
    diff_o = _attn_prompt(lams, dqb, dkb, vt, diff_g, nb, seq, lam_init)
    y_prompt = dense_tail(xp, gla_o, diff_o, 512, 1024).reshape(nb, seq, D_MODEL)

    xs = x_sample.reshape(ns, D_MODEL)
    sgq, sgk, sla, sgv, sgr, sdqb, sdk, sdv = _inproj(xs, g1, wm, wc, wa2p, ba, ns)
    sgla_o, s_sample = _gla_step(sgq, sgk, sla, sgv, sgr, gla_g, state_gla[0])
    sdiff_o = _attn_decode(page_table, lams, sdqb, sdk, sdv, diff_g, cache_k, cache_v, lam_init, 8)
    y_sample = dense_tail(xs, sgla_o.reshape(ns, GLA_V), sdiff_o.reshape(ns, DIFF_V), ns, ns).reshape(ns, 1, D_MODEL)

    k_prompt = dk.reshape(1, nb, seq, DIFF_HEADS, 2 * DIFF_HEAD_DIM)
    v_prompt = dv.reshape(1, nb, seq, DIFF_HEADS, DIFF_V_DIM)
    k_sample = sdk.reshape(1, ns, 1, DIFF_HEADS, 2 * DIFF_HEAD_DIM)
    v_sample = sdv.reshape(1, ns, 1, DIFF_HEADS, DIFF_V_DIM)
    return (y_prompt, y_sample, k_prompt, v_prompt, s_prompt[None], k_sample, v_sample, s_sample[None])
```

```python
import functools
import math

import jax
import jax.numpy as jnp
from jax import lax
from jax.experimental import pallas as pl
from jax.experimental.pallas import tpu as pltpu

F32 = jnp.float32
BF16 = jnp.bfloat16

D_MODEL = 1024
GLA_HEADS = 4
GLA_DK = 64
GLA_DV = 128
GLA_RANK = 16
GLA_TAU = 16.0
GLA_CHUNK = 16
DIFF_HEADS = 4
DIFF_HEAD_DIM = 64
DIFF_V_DIM = 128
GLA_QK = GLA_HEADS * GLA_DK
GLA_V = GLA_HEADS * GLA_DV
DIFF_QK = DIFF_HEADS * 2 * DIFF_HEAD_DIM
DIFF_V = DIFF_HEADS * DIFF_V_DIM
N_GROUPS = 4
EXPERTS_PER_GROUP = 4
N_EXPERTS = 16
EXPERT_FF = 256
EPS = 1e-6
LANES = 128
BF16_SUBLANES = 16
LOG2E = math.log2(math.e)
DECODE_SLOTS = 3
VT_ROWS = DIFF_V_DIM + BF16_SUBLANES
VMEM_LIMIT = 56 * 1024 * 1024

_C_GQ, _C_GK, _C_GV, _C_GR, _C_DQ, _C_DK, _C_DV, _C_END = 0, 256, 512, 1024, 1536, 2048, 2560, 3072


def _rms(x, g):
    return x * lax.rsqrt(jnp.mean(x * x, axis=-1, keepdims=True) + EPS) * g


def _silu(x):
    return x * (1.0 / (1.0 + jnp.exp(-x)))


def _log_sigmoid(x):
    return jnp.minimum(x, 0.0) - jnp.log1p(jnp.exp(-jnp.abs(x)))


def _dot(a, b):
    return jnp.dot(a, b, preferred_element_type=F32)


def _dot_nt(a, b):
    return lax.dot_general(a, b, (((1,), (1,)), ((), ())), preferred_element_type=F32)


def _split3(x):
    hi = x.astype(BF16)
    r = x - hi.astype(F32)
    mid = r.astype(BF16)
    lo = (r - mid.astype(F32)).astype(BF16)
    return hi, mid, lo


def _params(sem):
    return pltpu.CompilerParams(dimension_semantics=sem, vmem_limit_bytes=VMEM_LIMIT)


def _prep_w_in_kernel(w_ref, wm_ref, wc_ref):
    w = w_ref[0]
    c_code = _C_DQ
    wm_ref[:, :c_code] = w[:, :c_code].astype(BF16)
    wm_ref[:, c_code:] = w[:, c_code + GLA_RANK:].astype(BF16)
    lane = lax.broadcasted_iota(jnp.int32, (w.shape[0], LANES), 1)
    wc_ref[...] = jnp.where(lane < GLA_RANK, w[:, c_code:c_code + LANES], 0.0).astype(BF16)


def _prep_w_in(w_in, rows=128):
    d_in = w_in.shape[2]
    return pl.pallas_call(
        _prep_w_in_kernel,
        grid=(D_MODEL // rows,),
        in_specs=[pl.BlockSpec((1, rows, d_in), lambda i: (0, i, 0))],
        out_specs=[pl.BlockSpec((rows, _C_END), lambda i: (i, 0)), pl.BlockSpec((rows, LANES), lambda i: (i, 0))],
        out_shape=[jax.ShapeDtypeStruct((D_MODEL, _C_END), BF16), jax.ShapeDtypeStruct((D_MODEL, LANES), BF16)],
        compiler_params=_params(("parallel",)),
        name="prep_w_in",
    )(w_in)


def _inproj_kernel(x_ref, g_ref, wm_ref, wc_ref, wa2_ref, ba_ref,
                   gq_ref, gk_ref, la_ref, gv_ref, gr_ref, dqb_ref, dk_ref, dv_ref, *prompt_refs):
    h = _rms(x_ref[...], g_ref[...]).astype(BF16)

    def seg(lo, hi):
        return _dot(h, wm_ref[:, lo:hi])

    gq_ref[...] = seg(_C_GQ, _C_GK) * (GLA_DK ** -0.5)
    gk_ref[...] = seg(_C_GK, _C_GV)
    gv_ref[...] = seg(_C_GV, _C_GR)
    gr_ref[...] = seg(_C_GR, _C_DQ)
    dqb_ref[...] = (seg(_C_DQ, _C_DK) * (DIFF_HEAD_DIM ** -0.5 * LOG2E)).astype(BF16)
    dk = seg(_C_DK, _C_DV)
    dv = seg(_C_DV, _C_END)
    tm = dk.shape[0]
    for hd in range(DIFF_HEADS):
        rows = pl.ds(hd, tm, stride=DIFF_HEADS)
        dk_ref[rows, :] = dk[:, hd * LANES:(hd + 1) * LANES]
        dv_ref[rows, :] = dv[:, hd * LANES:(hd + 1) * LANES]
    if prompt_refs:
        dkb_ref, vt_ref = prompt_refs
        dkb_ref[...] = dk.astype(BF16)
        ones_row = (lax.broadcasted_iota(jnp.int32, (VT_ROWS - DIFF_V_DIM, tm), 0) == 0).astype(F32)
        for hd in range(DIFF_HEADS):
            vt_ref[0, hd, :DIFF_V_DIM, :] = dv[:, hd * DIFF_V_DIM:(hd + 1) * DIFF_V_DIM].T.astype(BF16)
            vt_ref[0, hd, DIFF_V_DIM:, :] = ones_row.astype(BF16)
    code = _dot(h, wc_ref[...])
    pre = _dot(code.astype(BF16), wa2_ref[...]) + ba_ref[...]
    la_ref[...] = _log_sigmoid(pre) * (1.0 / GLA_TAU)


def _inproj(x, g1, wm, wc, wa2p, ba, tm, prompt_shape=None):
    t = x.shape[0]
    row = lambda w: pl.BlockSpec((tm, w), lambda i: (i, 0))
    full = lambda a: pl.BlockSpec(a.shape, lambda i: (0,) * a.ndim)
    outs = [(GLA_QK, F32), (GLA_QK, F32), (GLA_QK, F32), (GLA_V, F32), (GLA_V, F32), (DIFF_QK, BF16)]
    out_specs = [row(w) for w, _ in outs]
    out_shape = [jax.ShapeDtypeStruct((t, w), dt) for w, dt in outs]
    out_specs += [pl.BlockSpec((tm * DIFF_HEADS, LANES), lambda i: (i, 0))] * 2
    out_shape += [jax.ShapeDtypeStruct((t * DIFF_HEADS, LANES), F32)] * 2
    if prompt_shape is not None:
        nb, seq = prompt_shape
        nl = seq // tm
        out_specs += [row(DIFF_QK), pl.BlockSpec((1, DIFF_HEADS, VT_ROWS, tm), lambda i: (i // nl, 0, 0, i % nl))]
        out_shape += [jax.ShapeDtypeStruct((t, DIFF_QK), BF16),
                      jax.ShapeDtypeStruct((nb, DIFF_HEADS, VT_ROWS, seq), BF16)]
    return pl.pallas_call(
        _inproj_kernel,
        grid=(t // tm,),
        in_specs=[row(D_MODEL), full(g1), full(wm), full(wc), full(wa2p), full(ba)],
        out_specs=out_specs,
        out_shape=out_shape,
        compiler_params=_params(("parallel",)),
        name="inproj",
    )(x, g1, wm, wc, wa2p, ba)


def _dot_tn(a, b):
    return lax.dot_general(a, b, (((0,), (0,)), ((), ())), preferred_element_type=F32)


def _gla_kernel(gq_ref, gk_ref, la_ref, gv_ref, gr_ref, tri_ref, last_ref, csel_ref, ind_ref, rep_ref, same_ref, g_ref, s0_ref,
                o_ref, sfin_ref,
                s_s, b_s, qi_s, kd_s, adec_s, o_s, p_s):
    l = pl.program_id(1)
    tl = gq_ref.shape[0]
    c16 = GLA_CHUNK

    @pl.when(l == 0)
    def _():
        for h in range(GLA_HEADS):
            s_s[h * GLA_DK:(h + 1) * GLA_DK, :] = s0_ref[0, h]

    hi, mid, lo = _split3(la_ref[...])

    def cs(m):
        return (_dot(m, lo) + _dot(m, mid)) + _dot(m, hi)

    b = cs(tri_ref[...])
    bl = cs(last_ref[...])
    b_s[...] = b
    qi_s[...] = gq_ref[...] * jnp.exp(b)
    kd_s[...] = (gk_ref[...] * jnp.exp(bl - b)).astype(BF16)
    csel = csel_ref[...]
    adec_s[...] = jnp.exp((_dot_tn(lo, csel) + _dot_tn(mid, csel)) + _dot_tn(hi, csel))

    rowi = lax.broadcasted_iota(jnp.int32, (c16, GLA_QK), 0)
    vb = gv_ref[...].astype(BF16)

    for c in range(tl // c16):
        r0 = c * c16
        q_c = gq_ref[r0:r0 + c16, :]
        b_c = b[r0:r0 + c16, :]
        for j in range(c16):
            bj = b_s[r0 + j:r0 + j + 1, :]
            kj = gk_ref[r0 + j:r0 + j + 1, :]
            e = jnp.exp(jnp.where(rowi >= j, b_c - bj, -jnp.inf))
            p_s[r0:r0 + c16, j * GLA_QK:(j + 1) * GLA_QK] = (q_c * e * kj).astype(BF16)
    scores = _dot(p_s[...], ind_ref[...]).astype(BF16)
    a_bd = _dot(scores, rep_ref[...]).astype(BF16) * same_ref[...]
    for h in range(GLA_HEADS):
        dv = slice(h * GLA_DV, (h + 1) * GLA_DV)
        o_s[:, dv] = _dot(a_bd[:, h * tl:(h + 1) * tl], vb[:, dv])

    adec = adec_s[...]
    state = [s_s[h * GLA_DK:(h + 1) * GLA_DK, :] for h in range(GLA_HEADS)]
    for c in range(tl // c16):
        rows = slice(c * c16, (c + 1) * c16)
        qi_c = qi_s[rows, :].astype(BF16)
        kd_c = kd_s[rows, :]
        vb_c = vb[rows, :]
        a_col = adec[:, c:c + 1]
        o_parts = []
        for h in range(GLA_HEADS):
            dk = slice(h * GLA_DK, (h + 1) * GLA_DK)
            dv = slice(h * GLA_DV, (h + 1) * GLA_DV)
            o_parts.append(_dot(qi_c[:, dk], state[h].astype(BF16)))
            state[h] = a_col[dk] * state[h] + _dot_tn(kd_c[:, dk], vb_c[:, dv])
        o_s[rows, :] += jnp.concatenate(o_parts, axis=1)
    for h in range(GLA_HEADS):
        s_s[h * GLA_DK:(h + 1) * GLA_DK, :] = state[h]

    o = o_s[...]
    for h in range(GLA_HEADS):
        sl = slice(h * GLA_DV, (h + 1) * GLA_DV)
        o_ref[:, sl] = (_rms(o[:, sl], g_ref[...]) * _silu(gr_ref[:, sl])).astype(o_ref.dtype)

    @pl.when(l == pl.num_programs(1) - 1)
    def _():
        for h in range(GLA_HEADS):
            sfin_ref[0, h] = s_s[h * GLA_DK:(h + 1) * GLA_DK, :]


def _gla_prompt(gq, gk, la, gv, gr, gla_g, s0, nb, tl):
    t = gq.shape[0]
    nl = t // nb // tl
    ti = jnp.arange(tl)
    same = (ti[:, None] // GLA_CHUNK) == (ti[None, :] // GLA_CHUNK)
    tri = (same & (ti[None, :] <= ti[:, None])).astype(BF16)
    last = same.astype(BF16)
    csel = ((ti[:, None] // GLA_CHUNK) == jnp.arange(LANES)[None, :]).astype(BF16)
    kj = jnp.arange(GLA_CHUNK * GLA_QK)
    ln = jnp.arange(LANES)
    ind = ((kj[:, None] // GLA_QK == ln[None, :] % GLA_CHUNK)
           & ((kj[:, None] % GLA_QK) // GLA_DK == ln[None, :] // GLA_CHUNK)).astype(BF16)
    ks = jnp.arange(GLA_HEADS * tl)
    rep = ((ln[:, None] // GLA_CHUNK == ks[None, :] // tl)
           & (ln[:, None] % GLA_CHUNK == ks[None, :] % GLA_CHUNK)).astype(BF16)
    same_chunk = (ti[:, None] // GLA_CHUNK == (ks[None, :] % tl) // GLA_CHUNK).astype(BF16)
    row = lambda w: pl.BlockSpec((tl, w), lambda b, l: (b * nl + l, 0))
    full = lambda shp: pl.BlockSpec(shp, lambda b, l: (0,) * len(shp))
    st = pl.BlockSpec((1, GLA_HEADS, GLA_DK, GLA_DV), lambda b, l: (b, 0, 0, 0))
    return pl.pallas_call(
        _gla_kernel,
        grid=(nb, nl),
        in_specs=[row(GLA_QK), row(GLA_QK), row(GLA_QK), row(GLA_V), row(GLA_V),
                  full((tl, tl)), full((tl, tl)), full((tl, LANES)), full(ind.shape), full(rep.shape), full(same_chunk.shape),
                  full((1, GLA_DV)), st],
        out_specs=[row(GLA_V), st],
        out_shape=[jax.ShapeDtypeStruct((t, GLA_V), BF16),
                   jax.ShapeDtypeStruct((nb, GLA_HEADS, GLA_DK, GLA_DV), F32)],
        scratch_shapes=[pltpu.VMEM((GLA_QK, GLA_DV), F32), pltpu.VMEM((tl, GLA_QK), F32),
                        pltpu.VMEM((tl, GLA_QK), F32), pltpu.VMEM((tl, GLA_QK), BF16),
                        pltpu.VMEM((GLA_QK, LANES), F32),
                        pltpu.VMEM((tl, GLA_V), F32), pltpu.VMEM((tl, GLA_CHUNK * GLA_QK), BF16)],
        compiler_params=_params(("parallel", "arbitrary")),
        name="gla_prompt",
    )(gq, gk, la, gv, gr, tri, last, csel, ind, rep, same_chunk, gla_g, s0)


def _gla_step_kernel(q_ref, k_ref, la_ref, v_ref, gr_ref, g_ref, s_ref, o_ref, snew_ref):
    eye = (lax.broadcasted_iota(jnp.int32, (GLA_QK, GLA_QK), 0)
           == lax.broadcasted_iota(jnp.int32, (GLA_QK, GLA_QK), 1))

    def column(row):
        return jnp.sum(jnp.where(eye, row, 0.0), axis=1, keepdims=True)

    for i in range(q_ref.shape[0]):
        q_col, k_col, a_col = column(q_ref[i]), column(k_ref[i]), jnp.exp(column(la_ref[i]))
        for h in range(GLA_HEADS):
            dk = slice(h * GLA_DK, (h + 1) * GLA_DK)
            dv = slice(h * GLA_DV, (h + 1) * GLA_DV)
            s_new = a_col[dk] * s_ref[i, h] + k_col[dk] * v_ref[i, :, dv]
            snew_ref[i, h] = s_new
            o = jnp.sum(q_col[dk] * s_new, axis=0, keepdims=True)
            o_ref[i, :, dv] = _rms(o, g_ref[...]) * _silu(gr_ref[i, :, dv])


def _gla_step(gq, gk, la, gv, gr, gla_g, state, per_step=8):
    n = gq.shape[0]
    assert n % per_step == 0
    col = pl.BlockSpec((per_step, 1, GLA_QK), lambda b: (b, 0, 0))
    rowv = pl.BlockSpec((per_step, 1, GLA_V), lambda b: (b, 0, 0))
    st = pl.BlockSpec((per_step, GLA_HEADS, GLA_DK, GLA_DV), lambda b: (b, 0, 0, 0))
    return pl.pallas_call(
        _gla_step_kernel,
        grid=(n // per_step,),
        in_specs=[col, col, col, rowv, rowv, pl.BlockSpec((1, GLA_DV), lambda b: (0, 0)), st],
        out_specs=[rowv, st],
        out_shape=[jax.ShapeDtypeStruct((n, 1, GLA_V), F32),
                   jax.ShapeDtypeStruct((n, GLA_HEADS, GLA_DK, GLA_DV), F32)],
        compiler_params=_params(("parallel",)),
        name="gla_step",
    )(gq.reshape(n, 1, GLA_QK), gk.reshape(n, 1, GLA_QK), la.reshape(n, 1, GLA_QK),
      gv.reshape(n, 1, GLA_V), gr.reshape(n, 1, GLA_V), gla_g, state)


def _lambda(lq1_ref, lk1_ref, lq2_ref, lk2_ref, lam_init):
    e1 = jnp.exp(jnp.sum(lq1_ref[...] * lk1_ref[...], axis=-1, keepdims=True))
    e2 = jnp.exp(jnp.sum(lq2_ref[...] * lk2_ref[...], axis=-1, keepdims=True))
    return e1 - e2 + lam_init


def _softmax_step(s, v, m, l, acc):
    m_new = jnp.maximum(m, jnp.max(s, axis=-1, keepdims=True))
    alpha = jnp.exp2(m - m_new)
    p = jnp.exp2(s - m_new)
    l = alpha * l + jnp.sum(p, axis=-1, keepdims=True)
    acc = alpha * acc + _dot(p.astype(BF16), v)
    return m_new, l, acc


def _attn_kernel(lq1_ref, lk1_ref, lq2_ref, lk2_ref, q_ref, k_ref, vt_ref, g_ref, o_ref, s_s, a_s,
                 *, lam_init, tq, ks, unroll):
    seq = q_ref.shape[0]
    lane = lax.broadcasted_iota(jnp.int32, (tq, LANES), 1)
    rel = lax.broadcasted_iota(jnp.int32, (ks, tq), 0) - lax.broadcasted_iota(jnp.int32, (ks, tq), 1)
    lam = _lambda(lq1_ref, lk1_ref, lq2_ref, lk2_ref, lam_init)
    n_mask = max(1, tq // ks)

    def q_tile(i, _):
        q0 = pl.multiple_of(i * tq, tq)
        q = q_ref[pl.ds(q0, tq), :]
        zero = jnp.zeros_like(q)
        qs = (jnp.where(lane < DIFF_HEAD_DIM, q, zero), jnp.where(lane >= DIFF_HEAD_DIM, q, zero))
        a_s[...] = jnp.zeros_like(a_s)
        n_full = (q0 + 1) // ks

        def score(j, slot, lo):
            k = k_ref[pl.ds(pl.multiple_of(j * ks, ks), ks), :]
            for mp in range(2):
                s_s[slot, mp, :, lo:] = _dot_nt(k, qs[mp][lo:])

        def step(j, cur, ms, masked, lo=0, lo_next=0):
            if lo_next is not None:
                score(j + 1, 1 - cur, lo_next)
            k0 = pl.multiple_of(j * ks, ks)
            vt = vt_ref[0, 0, :, pl.ds(k0, ks)]
            out = []
            for mp in range(2):
                s = s_s[cur, mp, :, lo:]
                if masked:
                    s = jnp.where(rel[:, lo:] <= q0 - k0, s, -jnp.inf)
                m_old = ms[mp][:, lo:]
                m_new = jnp.maximum(m_old, jnp.max(s, axis=0, keepdims=True))
                p = jnp.exp2(s - m_new).astype(BF16)
                a_s[mp, :, lo:] = jnp.exp2(m_old - m_new) * a_s[mp, :, lo:] + _dot(vt, p)
                out.append(m_new if lo == 0 else jnp.concatenate([ms[mp][:, :lo], m_new], axis=1))
            return tuple(out)

        minf = jnp.full((1, tq), -jnp.inf, F32)
        score(0, 0, 0)

        def trip(jj, ms):
            for u in range(unroll):
                ms = step(unroll * jj + u, u % 2, ms, False)
            return ms

        ms = lax.fori_loop(0, n_full // unroll, trip, (minf, minf))
        for r in range(n_mask):
            lo_next = (r + 1) * ks if r + 1 < n_mask else None
            ms = step(n_full + r, r % 2, ms, True, r * ks, lo_next)
        a1, a2 = a_s[0], a_s[1]
        w = a1[:DIFF_V_DIM] / a1[DIFF_V_DIM:DIFF_V_DIM + 1] - lam * (a2[:DIFF_V_DIM] / a2[DIFF_V_DIM:DIFF_V_DIM + 1])
        o_ref[pl.ds(q0, tq), :] = (_rms(w.T, g_ref[...]) * (1.0 - lam_init)).astype(o_ref.dtype)
        return 0

    lax.fori_loop(0, seq // tq, q_tile, 0)


def _attn_prompt(lams, dqb, dkb, vt, diff_g, nb, seq, lam_init, tq=1024, ks=256, unroll=4):
    t = dqb.shape[0]
    assert unroll % 2 == 0 and tq % (unroll * ks) == 0 and seq % tq == 0
    lam_spec = pl.BlockSpec((1, DIFF_HEAD_DIM), lambda b, h: (0, 0))
    seqspec = pl.BlockSpec((seq, LANES), lambda b, h: (b, h))
    return pl.pallas_call(
        functools.partial(_attn_kernel, lam_init=lam_init, tq=tq, ks=ks, unroll=unroll),
        grid=(nb, DIFF_HEADS),
        in_specs=[lam_spec] * 4 + [seqspec, seqspec, pl.BlockSpec((1, 1, VT_ROWS, seq), lambda b, h: (b, h, 0, 0)),
                                  pl.BlockSpec((1, DIFF_V_DIM), lambda b, h: (0, 0))],
        out_specs=seqspec,
        out_shape=jax.ShapeDtypeStruct((t, DIFF_V), BF16),
        scratch_shapes=[pltpu.VMEM((2, 2, ks, tq), F32), pltpu.VMEM((2, VT_ROWS, tq), F32)],
        compiler_params=_params(("parallel", "parallel")),
        name="attn_prompt",
    )(*lams, dqb, dkb, vt, diff_g)


def _decode_kernel(pt_ref, lq1_ref, lk1_ref, lq2_ref, lk2_ref, q_ref, kn_ref, vn_ref, g_ref, ck_ref, cv_ref,
                   o_ref, kbuf, vbuf, sem, m_s, l_s, acc_s, *, lam_init, ppc, nch, nseq, page_rows):
    b = pl.program_id(0)
    c = pl.program_id(1)
    step = b * nch + c
    slot = lax.rem(step, DECODE_SLOTS)
    nrow = 2 * DIFF_HEADS

    def copies(n):
        bb, cc, sl = n // nch, lax.rem(n, nch), lax.rem(n, DECODE_SLOTS)
        out = []
        for p in range(ppc):
            page = pt_ref[bb, cc * ppc + p]
            dst = pl.ds(p * page_rows, page_rows)
            out.append(pltpu.make_async_copy(ck_ref.at[page], kbuf.at[sl, dst, :], sem.at[0, sl]))
            out.append(pltpu.make_async_copy(cv_ref.at[page], vbuf.at[sl, dst, :], sem.at[1, sl]))
        return out

    ahead = DECODE_SLOTS - 1

    @pl.when(step == 0)
    def _():
        for n in range(ahead):
            for cp in copies(jnp.int32(n)):
                cp.start()

    @pl.when(step + ahead < nseq * nch)
    def _():
        for cp in copies(step + ahead):
            cp.start()

    @pl.when(c == 0)
    def _():
        m_s[...] = jnp.full_like(m_s, -jnp.inf)
        l_s[...] = jnp.zeros_like(l_s)
        acc_s[...] = jnp.zeros_like(acc_s)

    for cp in copies(step):
        cp.wait()

    rowi = lax.broadcasted_iota(jnp.int32, (nrow, LANES), 0)
    lane = lax.broadcasted_iota(jnp.int32, (nrow, LANES), 1)
    q4 = q_ref[0].astype(F32)
    q8 = jnp.concatenate([q4, q4], axis=0)
    qm = jnp.where((rowi < DIFF_HEADS) == (lane < DIFF_HEAD_DIM), q8, 0.0)

    keys = ppc * page_rows
    s = _dot_nt(qm.astype(BF16), kbuf[slot].astype(BF16))
    col = lax.broadcasted_iota(jnp.int32, (nrow, keys), 1)
    rowk = lax.broadcasted_iota(jnp.int32, (nrow, keys), 0)
    s = jnp.where((col % DIFF_HEADS) == (rowk % DIFF_HEADS), s, -jnp.inf)
    m, l, acc = _softmax_step(s, vbuf[slot].astype(BF16), m_s[...], l_s[...], acc_s[...])
    m_s[...] = m
    l_s[...] = l
    acc_s[...] = acc

    @pl.when(c == nch - 1)
    def _():
        lam = _lambda(lq1_ref, lk1_ref, lq2_ref, lk2_ref, lam_init)
        kn = kn_ref[0].astype(BF16).astype(F32)
        vn = vn_ref[0].astype(BF16).astype(F32)
        sn = jnp.sum(qm * jnp.concatenate([kn, kn], axis=0), axis=-1, keepdims=True)
        m_new = jnp.maximum(m, sn)
        alpha = jnp.exp2(m - m_new)
        p = jnp.exp2(sn - m_new)
        lf = alpha * l + p
        w = (alpha * acc + p.astype(BF16).astype(F32) * jnp.concatenate([vn, vn], axis=0)) / lf
        out = w[:DIFF_HEADS] - lam * w[DIFF_HEADS:]
        o_ref[0] = _rms(out, g_ref[...]) * (1.0 - lam_init)


def _attn_decode(page_table, lams, dqb, dk, dv, diff_g, cache_k, cache_v, lam_init, ppc):
    n, n_pages = page_table.shape
    nch = n_pages // ppc
    n_pool, page = cache_k.shape[1], cache_k.shape[2]
    page_rows = page * DIFF_HEADS
    ck = cache_k.reshape(n_pool, page_rows, 2 * DIFF_HEAD_DIM)
    cv = cache_v.reshape(n_pool, page_rows, DIFF_V_DIM)
    lam_spec = pl.BlockSpec((1, DIFF_HEAD_DIM), lambda b, c, pt: (0, 0))
    rowspec = pl.BlockSpec((1, DIFF_HEADS, LANES), lambda b, c, pt: (b, 0, 0))
    anyspec = pl.BlockSpec(memory_space=pl.ANY)
    kern = functools.partial(_decode_kernel, lam_init=lam_init, ppc=ppc, nch=nch, nseq=n, page_rows=page_rows)
    nrow = 2 * DIFF_HEADS
    return pl.pallas_call(
        kern,
        grid_spec=pltpu.PrefetchScalarGridSpec(
            num_scalar_prefetch=1,
            grid=(n, nch),
            in_specs=[lam_spec] * 4 + [rowspec, rowspec, rowspec,
                                      pl.BlockSpec((1, DIFF_V_DIM), lambda b, c, pt: (0, 0)), anyspec, anyspec],
            out_specs=rowspec,
            scratch_shapes=[pltpu.VMEM((DECODE_SLOTS, ppc * page_rows, LANES), F32),
                            pltpu.VMEM((DECODE_SLOTS, ppc * page_rows, LANES), F32),
                            pltpu.SemaphoreType.DMA((2, DECODE_SLOTS)),
                            pltpu.VMEM((nrow, 1), F32), pltpu.VMEM((nrow, 1), F32),
                            pltpu.VMEM((nrow, DIFF_V_DIM), F32)]),
        out_shape=jax.ShapeDtypeStruct((n, DIFF_HEADS, LANES), F32),
        compiler_params=_params(("arbitrary", "arbitrary")),
        name="attn_decode",
    )(page_table, *lams, dqb.reshape(n, DIFF_HEADS, LANES), dk.reshape(n, DIFF_HEADS, LANES),
      dv.reshape(n, DIFF_HEADS, LANES), diff_g, ck, cv)


def _post_kernel(x_ref, go_ref, do_ref, wo_ref, g2_ref, wrh_ref, wrl_ref, br_ref, x1_ref, t_ref, gates_ref):
    x1 = x_ref[...] + _dot(go_ref[...].astype(BF16), wo_ref[:GLA_V, :]) + _dot(do_ref[...].astype(BF16), wo_ref[GLA_V:, :])
    x1_ref[...] = x1
    t = _rms(x1, g2_ref[...])
    t_ref[...] = t.astype(BF16)
    t_hi = t.astype(BF16)
    t_lo = (t - t_hi.astype(F32)).astype(BF16)
    w_hi = wrh_ref[...]
    w_lo = wrl_ref[...]
    logits = (_dot(t_lo, w_hi) + _dot(t_hi, w_lo)) + _dot(t_hi, w_hi) + br_ref[...]
    lane = lax.broadcasted_iota(jnp.int32, logits.shape, 1)
    neg = -jnp.inf
    is_group = (lane >= N_EXPERTS) & (lane < N_EXPERTS + N_GROUPS)
    lg = jnp.where(is_group, logits, neg)
    lg_max = jnp.max(lg, axis=-1, keepdims=True)
    g_lane = jnp.min(jnp.where(lg == lg_max, lane, LANES), axis=-1, keepdims=True)
    p_top = 1.0 / jnp.sum(jnp.exp(lg - lg_max), axis=-1, keepdims=True)
    in_g = (lane // EXPERTS_PER_GROUP) == (g_lane - N_EXPERTS)
    le = jnp.where(in_g, logits, neg)
    v1 = jnp.max(le, axis=-1, keepdims=True)
    i1 = jnp.min(jnp.where(le == v1, lane, LANES), axis=-1, keepdims=True)
    le2 = jnp.where(lane == i1, neg, le)
    v2 = jnp.max(le2, axis=-1, keepdims=True)
    i2 = jnp.min(jnp.where(le2 == v2, lane, LANES), axis=-1, keepdims=True)
    e2 = jnp.exp(v2 - v1)
    w1 = p_top / (1.0 + e2)
    w2 = p_top * e2 / (1.0 + e2)
    gates_ref[...] = jnp.where(lane == i1, w1, 0.0) + jnp.where(lane == i2, w2, 0.0)


def _post(x, gla_o, diff_o, wo, g2, wr_hi, wr_lo, br, tm):
    t = x.shape[0]
    row = lambda w: pl.BlockSpec((tm, w), lambda i: (i, 0))
    full = lambda a: pl.BlockSpec(a.shape, lambda i: (0,) * a.ndim)
    return pl.pallas_call(
        _post_kernel,
        grid=(t // tm,),
        in_specs=[row(D_MODEL), row(GLA_V), row(DIFF_V), full(wo), full(g2), full(wr_hi), full(wr_lo), full(br)],
        out_specs=[row(D_MODEL), row(D_MODEL), row(LANES)],
        out_shape=[jax.ShapeDtypeStruct((t, D_MODEL), F32), jax.ShapeDtypeStruct((t, D_MODEL), BF16),
                   jax.ShapeDtypeStruct((t, LANES), F32)],
        compiler_params=_params(("parallel",)),
        name="post",
    )(x, gla_o, diff_o, wo, g2, wr_hi, wr_lo, br)


def _moe_kernel(t_ref, gates_ref, x1_ref, wgu_ref, wd_ref, gf_ref, y_ref, acc_s, *, eps_per_step):
    e0 = pl.program_id(1)

    @pl.when(e0 == 0)
    def _():
        acc_s[...] = jnp.zeros_like(acc_s)

    t = t_ref[...]
    gates = gates_ref[...]
    ei = lax.broadcasted_iota(jnp.int32, gates.shape, 1)
    acc = acc_s[...]
    for k in range(eps_per_step):
        gu = _dot(t, wgu_ref[k])
        gate = jnp.sum(jnp.where(ei == e0 * eps_per_step + k, gates, 0.0), axis=-1, keepdims=True)
        hid = _silu(gu[:, :EXPERT_FF]) * gu[:, EXPERT_FF:] * gate
        acc = acc + _dot(hid.astype(BF16), wd_ref[k])
    acc_s[...] = acc

    @pl.when(e0 == pl.num_programs(1) - 1)
    def _():
        y_ref[...] = _rms(x1_ref[...] + acc_s[...], gf_ref[...])


def _moe(t, gates, x1, wgu, wd, gf, tm, eps_per_step=EXPERTS_PER_GROUP):
    n = t.shape[0]
    row = lambda w: pl.BlockSpec((tm, w), lambda i, e: (i, 0))
    return pl.pallas_call(
        functools.partial(_moe_kernel, eps_per_step=eps_per_step),
        grid=(n // tm, N_EXPERTS // eps_per_step),
        in_specs=[row(D_MODEL), row(LANES), row(D_MODEL),
                  pl.BlockSpec((eps_per_step, D_MODEL, 2 * EXPERT_FF), lambda i, e: (e, 0, 0)),
                  pl.BlockSpec((eps_per_step, EXPERT_FF, D_MODEL), lambda i, e: (e, 0, 0)),
                  pl.BlockSpec((1, D_MODEL), lambda i, e: (0, 0))],
        out_specs=row(D_MODEL),
        out_shape=jax.ShapeDtypeStruct((n, D_MODEL), F32),
        scratch_shapes=[pltpu.VMEM((tm, D_MODEL), F32)],
        compiler_params=_params(("parallel", "arbitrary")),
        name="moe",
    )(t, gates, x1, wgu, wd, gf)


def kernel(x_prompt, x_sample, cache_k, cache_v, state_gla, page_table, norm1_g, w_in, w_a2, b_a, gla_norm_g,
           lambda_q1, lambda_k1, lambda_q2, lambda_k2, diff_norm_g, w_out, norm2_g, w_router_g, b_router_g,
           w_router_e, b_router_e, w_gate, w_up, w_down, norm_f_g):
    nb, seq, _ = x_prompt.shape
    ns = x_sample.shape[0]
    depth = w_in.shape[0]
    assert depth == 1 and x_sample.shape[1] == 1
    lam_init = 0.8 - 0.6 * math.exp(-0.3 * 0)

    wm, wc = _prep_w_in(w_in)
    wa2p = jnp.pad(w_a2[0], ((0, LANES - GLA_RANK), (0, 0))).astype(BF16)
    ba = b_a[0].reshape(1, GLA_QK)
    g1 = norm1_g[0].reshape(1, D_MODEL)
    g2 = norm2_g[0].reshape(1, D_MODEL)
    gf = norm_f_g.reshape(1, D_MODEL)
    gla_g = gla_norm_g[0].reshape(1, GLA_DV)
    diff_g = diff_norm_g[0].reshape(1, DIFF_V_DIM)
    lams = [a[0].reshape(1, DIFF_HEAD_DIM) for a in (lambda_q1, lambda_k1, lambda_q2, lambda_k2)]
    wo = w_out[0].astype(BF16)
    wr = jnp.concatenate([w_router_e[0].transpose(1, 0, 2).reshape(D_MODEL, N_EXPERTS), w_router_g[0]], axis=1)
    wr = jnp.pad(wr, ((0, 0), (0, LANES - N_GROUPS - N_EXPERTS)))
    wr_hi = wr.astype(BF16)
    wr_lo = (wr - wr_hi.astype(F32)).astype(BF16)
    br = jnp.pad(jnp.concatenate([b_router_e[0].reshape(N_EXPERTS), b_router_g[0]]),
                 (0, LANES - N_GROUPS - N_EXPERTS)).reshape(1, LANES)
    wgu = jnp.concatenate([w_gate[0], w_up[0]], axis=-1).astype(BF16)
    wd = w_down[0].astype(BF16)

    def dense_tail(x, gla_o, diff_o, tm_post, tm_moe):
        x1, t, gates = _post(x, gla_o, diff_o, wo, g2, wr_hi, wr_lo, br, tm_post)
        return _moe(t, gates, x1, wgu, wd, gf, tm_moe)

    xp = x_prompt.reshape(nb * seq, D_MODEL)
    gq, gk, la, gv, gr, dqb, dk, dv, dkb, vt = _inproj(xp, g1, wm, wc, wa2p, ba, 512, (nb, seq))
    s0 = jnp.zeros((nb, GLA_HEADS, GLA_DK, GLA_DV), F32)
    gla_o, s_prompt = _gla_prompt(gq, gk, la, gv, gr, gla_g, s0, nb, 256)
    diff_o = _attn_prompt(lams, dqb, dkb, vt, diff_g, nb, seq, lam_init)
    y_prompt = dense_tail(xp, gla_o, diff_o, 512, 1024).reshape(nb, seq, D_MODEL)

    xs = x_sample.reshape(ns, D_MODEL)
    sgq, sgk, sla, sgv, sgr, sdqb, sdk, sdv = _inproj(xs, g1, wm, wc, wa2p, ba, ns)
    sgla_o, s_sample = _gla_step(sgq, sgk, sla, sgv, sgr, gla_g, state_gla[0])
    sdiff_o = _attn_decode(page_table, lams, sdqb, sdk, sdv, diff_g, cache_k, cache_v, lam_init, 8)
    y_sample = dense_tail(xs, sgla_o.reshape(ns, GLA_V), sdiff_o.reshape(ns, DIFF_V), ns, ns).reshape(ns, 1, D_MODEL)

    k_prompt = dk.reshape(1, nb, seq, DIFF_HEADS, 2 * DIFF_HEAD_DIM)
    v_prompt = dv.reshape(1, nb, seq, DIFF_HEADS, DIFF_V_DIM)
    k_sample = sdk.reshape(1, ns, 1, DIFF_HEADS, 2 * DIFF_HEAD_DIM)
    v_sample = sdv.reshape(1, ns, 1, DIFF_HEADS, DIFF_V_DIM)
    return (y_prompt, y_sample, k_prompt, v_prompt, s_prompt[None], k_sample, v_sample, s_sample[None])
```

```python
import functools
import math

import jax
import jax.numpy as jnp
from jax import lax
from jax.experimental import pallas as pl
from jax.experimental.pallas import tpu as pltpu

F32 = jnp.float32
BF16 = jnp.bfloat16

D_MODEL = 1024
GLA_HEADS = 4
GLA_DK = 64
GLA_DV = 128
GLA_RANK = 16
GLA_TAU = 16.0
GLA_CHUNK = 16
DIFF_HEADS = 4
DIFF_HEAD_DIM = 64
DIFF_V_DIM = 128
GLA_QK = GLA_HEADS * GLA_DK
GLA_V = GLA_HEADS * GLA_DV
DIFF_QK = DIFF_HEADS * 2 * DIFF_HEAD_DIM
DIFF_V = DIFF_HEADS * DIFF_V_DIM
N_GROUPS = 4
EXPERTS_PER_GROUP = 4
N_EXPERTS = 16
EXPERT_FF = 256
EPS = 1e-6
LANES = 128
BF16_SUBLANES = 16
LOG2E = math.log2(math.e)
DECODE_SLOTS = 3
VT_ROWS = DIFF_V_DIM + BF16_SUBLANES
VMEM_LIMIT = 56 * 1024 * 1024

_C_GQ, _C_GK, _C_GV, _C_GR, _C_DQ, _C_DK, _C_DV, _C_END = 0, 256, 512, 1024, 1536, 2048, 2560, 3072


def _rms(x, g):
    return x * lax.rsqrt(jnp.mean(x * x, axis=-1, keepdims=True) + EPS) * g


def _silu(x):
    return x * (1.0 / (1.0 + jnp.exp(-x)))


def _log_sigmoid(x):
    return jnp.minimum(x, 0.0) - jnp.log1p(jnp.exp(-jnp.abs(x)))


def _dot(a, b):
    return jnp.dot(a, b, preferred_element_type=F32)


def _dot_nt(a, b):
    return lax.dot_general(a, b, (((1,), (1,)), ((), ())), preferred_element_type=F32)


def _split3(x):
    hi = x.astype(BF16)
    r = x - hi.astype(F32)
    mid = r.astype(BF16)
    lo = (r - mid.astype(F32)).astype(BF16)
    return hi, mid, lo


def _params(sem):
    return pltpu.CompilerParams(dimension_semantics=sem, vmem_limit_bytes=VMEM_LIMIT)


def _inproj_kernel(x_ref, g_ref, wm_ref, wc_ref, wa2_ref, ba_ref,
                   gq_ref, gk_ref, la_ref, gv_ref, gr_ref, dqb_ref, dk_ref, dv_ref, *prompt_refs):
    h = _rms(x_ref[...], g_ref[...]).astype(BF16)

    def seg(lo, hi):
        return _dot(h, wm_ref[:, lo:hi])

    gq_ref[...] = seg(_C_GQ, _C_GK) * (GLA_DK ** -0.5)
    gk_ref[...] = seg(_C_GK, _C_GV)
    gv_ref[...] = seg(_C_GV, _C_GR)
    gr_ref[...] = seg(_C_GR, _C_DQ)
    dqb_ref[...] = (seg(_C_DQ, _C_DK) * (DIFF_HEAD_DIM ** -0.5 * LOG2E)).astype(BF16)
    dk = seg(_C_DK, _C_DV)
    dv = seg(_C_DV, _C_END)
    tm = dk.shape[0]
    for hd in range(DIFF_HEADS):
        rows = pl.ds(hd, tm, stride=DIFF_HEADS)
        dk_ref[rows, :] = dk[:, hd * LANES:(hd + 1) * LANES]
        dv_ref[rows, :] = dv[:, hd * LANES:(hd + 1) * LANES]
    if prompt_refs:
        dkb_ref, vt_ref = prompt_refs
        dkb_ref[...] = dk.astype(BF16)
        ones_row = (lax.broadcasted_iota(jnp.int32, (VT_ROWS - DIFF_V_DIM, tm), 0) == 0).astype(F32)
        for hd in range(DIFF_HEADS):
            vt_ref[0, hd, :DIFF_V_DIM, :] = dv[:, hd * DIFF_V_DIM:(hd + 1) * DIFF_V_DIM].T.astype(BF16)
            vt_ref[0, hd, DIFF_V_DIM:, :] = ones_row.astype(BF16)
    code = _dot(h, wc_ref[...])
    pre = _dot(code.astype(BF16), wa2_ref[...]) + ba_ref[...]
    la_ref[...] = _log_sigmoid(pre) * (1.0 / GLA_TAU)


def _inproj(x, g1, wm, wc, wa2p, ba, tm, prompt_shape=None):
    t = x.shape[0]
    row = lambda w: pl.BlockSpec((tm, w), lambda i: (i, 0))
    full = lambda a: pl.BlockSpec(a.shape, lambda i: (0,) * a.ndim)
    outs = [(GLA_QK, F32), (GLA_QK, F32), (GLA_QK, F32), (GLA_V, F32), (GLA_V, F32), (DIFF_QK, BF16)]
    out_specs = [row(w) for w, _ in outs]
    out_shape = [jax.ShapeDtypeStruct((t, w), dt) for w, dt in outs]
    out_specs += [pl.BlockSpec((tm * DIFF_HEADS, LANES), lambda i: (i, 0))] * 2
    out_shape += [jax.ShapeDtypeStruct((t * DIFF_HEADS, LANES), F32)] * 2
    if prompt_shape is not None:
        nb, seq = prompt_shape
        nl = seq // tm
        out_specs += [row(DIFF_QK), pl.BlockSpec((1, DIFF_HEADS, VT_ROWS, tm), lambda i: (i // nl, 0, 0, i % nl))]
        out_shape += [jax.ShapeDtypeStruct((t, DIFF_QK), BF16),
                      jax.ShapeDtypeStruct((nb, DIFF_HEADS, VT_ROWS, seq), BF16)]
    return pl.pallas_call(
        _inproj_kernel,
        grid=(t // tm,),
        in_specs=[row(D_MODEL), full(g1), full(wm), full(wc), full(wa2p), full(ba)],
        out_specs=out_specs,
        out_shape=out_shape,
        compiler_params=_params(("parallel",)),
        name="inproj",
    )(x, g1, wm, wc, wa2p, ba)


def _dot_tn(a, b):
    return lax.dot_general(a, b, (((0,), (0,)), ((), ())), preferred_element_type=F32)


def _gla_kernel(gq_ref, gk_ref, la_ref, gv_ref, gr_ref, tri_ref, last_ref, csel_ref, ind_ref, rep_ref, same_ref, g_ref, s0_ref,
                o_ref, sfin_ref,
                s_s, b_s, qi_s, kd_s, adec_s, o_s, p_s):
    l = pl.program_id(1)
    tl = gq_ref.shape[0]
    c16 = GLA_CHUNK

    @pl.when(l == 0)
    def _():
        for h in range(GLA_HEADS):
            s_s[h * GLA_DK:(h + 1) * GLA_DK, :] = s0_ref[0, h]

    hi, mid, lo = _split3(la_ref[...])

    def cs(m):
        return (_dot(m, lo) + _dot(m, mid)) + _dot(m, hi)

    b = cs(tri_ref[...])
    bl = cs(last_ref[...])
    b_s[...] = b
    qi_s[...] = gq_ref[...] * jnp.exp(b)
    kd_s[...] = (gk_ref[...] * jnp.exp(bl - b)).astype(BF16)
    csel = csel_ref[...]
    adec_s[...] = jnp.exp((_dot_tn(lo, csel) + _dot_tn(mid, csel)) + _dot_tn(hi, csel))

    rowi = lax.broadcasted_iota(jnp.int32, (c16, GLA_QK), 0)
    vb = gv_ref[...].astype(BF16)

    for c in range(tl // c16):
        r0 = c * c16
        q_c = gq_ref[r0:r0 + c16, :]
        b_c = b[r0:r0 + c16, :]
        for j in range(c16):
            bj = b_s[r0 + j:r0 + j + 1, :]
            kj = gk_ref[r0 + j:r0 + j + 1, :]
            e = jnp.exp(jnp.where(rowi >= j, b_c - bj, -jnp.inf))
            p_s[r0:r0 + c16, j * GLA_QK:(j + 1) * GLA_QK] = (q_c * e * kj).astype(BF16)
    scores = _dot(p_s[...], ind_ref[...]).astype(BF16)
    a_bd = _dot(scores, rep_ref[...]).astype(BF16) * same_ref[...]
    for h in range(GLA_HEADS):
        dv = slice(h * GLA_DV, (h + 1) * GLA_DV)
        o_s[:, dv] = _dot(a_bd[:, h * tl:(h + 1) * tl], vb[:, dv])

    adec = adec_s[...]
    state = [s_s[h * GLA_DK:(h + 1) * GLA_DK, :] for h in range(GLA_HEADS)]
    for c in range(tl // c16):
        rows = slice(c * c16, (c + 1) * c16)
        qi_c = qi_s[rows, :].astype(BF16)
        kd_c = kd_s[rows, :]
        vb_c = vb[rows, :]
        a_col = adec[:, c:c + 1]
        o_parts = []
        for h in range(GLA_HEADS):
            dk = slice(h * GLA_DK, (h + 1) * GLA_DK)
            dv = slice(h * GLA_DV, (h + 1) * GLA_DV)
            o_parts.append(_dot(qi_c[:, dk], state[h].astype(BF16)))
            state[h] = a_col[dk] * state[h] + _dot_tn(kd_c[:, dk], vb_c[:, dv])
        o_s[rows, :] += jnp.concatenate(o_parts, axis=1)
    for h in range(GLA_HEADS):
        s_s[h * GLA_DK:(h + 1) * GLA_DK, :] = state[h]

    o = o_s[...]
    for h in range(GLA_HEADS):
        sl = slice(h * GLA_DV, (h + 1) * GLA_DV)
        o_ref[:, sl] = (_rms(o[:, sl], g_ref[...]) * _silu(gr_ref[:, sl])).astype(o_ref.dtype)

    @pl.when(l == pl.num_programs(1) - 1)
    def _():
        for h in range(GLA_HEADS):
            sfin_ref[0, h] = s_s[h * GLA_DK:(h + 1) * GLA_DK, :]


def _gla_prompt(gq, gk, la, gv, gr, gla_g, s0, nb, tl):
    t = gq.shape[0]
    nl = t // nb // tl
    ti = jnp.arange(tl)
    same = (ti[:, None] // GLA_CHUNK) == (ti[None, :] // GLA_CHUNK)
    tri = (same & (ti[None, :] <= ti[:, None])).astype(BF16)
    last = same.astype(BF16)
    csel = ((ti[:, None] // GLA_CHUNK) == jnp.arange(LANES)[None, :]).astype(BF16)
    kj = jnp.arange(GLA_CHUNK * GLA_QK)
    ln = jnp.arange(LANES)
    ind = ((kj[:, None] // GLA_QK == ln[None, :] % GLA_CHUNK)
           & ((kj[:, None] % GLA_QK) // GLA_DK == ln[None, :] // GLA_CHUNK)).astype(BF16)
    ks = jnp.arange(GLA_HEADS * tl)
    rep = ((ln[:, None] // GLA_CHUNK == ks[None, :] // tl)
           & (ln[:, None] % GLA_CHUNK == ks[None, :] % GLA_CHUNK)).astype(BF16)
    same_chunk = (ti[:, None] // GLA_CHUNK == (ks[None, :] % tl) // GLA_CHUNK).astype(BF16)
    row = lambda w: pl.BlockSpec((tl, w), lambda b, l: (b * nl + l, 0))
    full = lambda shp: pl.BlockSpec(shp, lambda b, l: (0,) * len(shp))
    st = pl.BlockSpec((1, GLA_HEADS, GLA_DK, GLA_DV), lambda b, l: (b, 0, 0, 0))
    return pl.pallas_call(
        _gla_kernel,
        grid=(nb, nl),
        in_specs=[row(GLA_QK), row(GLA_QK), row(GLA_QK), row(GLA_V), row(GLA_V),
                  full((tl, tl)), full((tl, tl)), full((tl, LANES)), full(ind.shape), full(rep.shape), full(same_chunk.shape),
                  full((1, GLA_DV)), st],
        out_specs=[row(GLA_V), st],
        out_shape=[jax.ShapeDtypeStruct((t, GLA_V), BF16),
                   jax.ShapeDtypeStruct((nb, GLA_HEADS, GLA_DK, GLA_DV), F32)],
        scratch_shapes=[pltpu.VMEM((GLA_QK, GLA_DV), F32), pltpu.VMEM((tl, GLA_QK), F32),
                        pltpu.VMEM((tl, GLA_QK), F32), pltpu.VMEM((tl, GLA_QK), BF16),
                        pltpu.VMEM((GLA_QK, LANES), F32),
                        pltpu.VMEM((tl, GLA_V), F32), pltpu.VMEM((tl, GLA_CHUNK * GLA_QK), BF16)],
        compiler_params=_params(("parallel", "arbitrary")),
        name="gla_prompt",
    )(gq, gk, la, gv, gr, tri, last, csel, ind, rep, same_chunk, gla_g, s0)


def _gla_step_kernel(q_ref, k_ref, la_ref, v_ref, gr_ref, g_ref, s_ref, o_ref, snew_ref):
    eye = (lax.broadcasted_iota(jnp.int32, (GLA_QK, GLA_QK), 0)
           == lax.broadcasted_iota(jnp.int32, (GLA_QK, GLA_QK), 1))

    def column(row):
        return jnp.sum(jnp.where(eye, row, 0.0), axis=1, keepdims=True)

    for i in range(q_ref.shape[0]):
        q_col, k_col, a_col = column(q_ref[i]), column(k_ref[i]), jnp.exp(column(la_ref[i]))
        for h in range(GLA_HEADS):
            dk = slice(h * GLA_DK, (h + 1) * GLA_DK)
            dv = slice(h * GLA_DV, (h + 1) * GLA_DV)
            s_new = a_col[dk] * s_ref[i, h] + k_col[dk] * v_ref[i, :, dv]
            snew_ref[i, h] = s_new
            o = jnp.sum(q_col[dk] * s_new, axis=0, keepdims=True)
            o_ref[i, :, dv] = _rms(o, g_ref[...]) * _silu(gr_ref[i, :, dv])


def _gla_step(gq, gk, la, gv, gr, gla_g, state, per_step=8):
    n = gq.shape[0]
    assert n % per_step == 0
    col = pl.BlockSpec((per_step, 1, GLA_QK), lambda b: (b, 0, 0))
    rowv = pl.BlockSpec((per_step, 1, GLA_V), lambda b: (b, 0, 0))
    st = pl.BlockSpec((per_step, GLA_HEADS, GLA_DK, GLA_DV), lambda b: (b, 0, 0, 0))
    return pl.pallas_call(
        _gla_step_kernel,
        grid=(n // per_step,),
        in_specs=[col, col, col, rowv, rowv, pl.BlockSpec((1, GLA_DV), lambda b: (0, 0)), st],
        out_specs=[rowv, st],
        out_shape=[jax.ShapeDtypeStruct((n, 1, GLA_V), F32),
                   jax.ShapeDtypeStruct((n, GLA_HEADS, GLA_DK, GLA_DV), F32)],
        compiler_params=_params(("parallel",)),
        name="gla_step",
    )(gq.reshape(n, 1, GLA_QK), gk.reshape(n, 1, GLA_QK), la.reshape(n, 1, GLA_QK),
      gv.reshape(n, 1, GLA_V), gr.reshape(n, 1, GLA_V), gla_g, state)


def _lambda(lq1_ref, lk1_ref, lq2_ref, lk2_ref, lam_init):
    e1 = jnp.exp(jnp.sum(lq1_ref[...] * lk1_ref[...], axis=-1, keepdims=True))
    e2 = jnp.exp(jnp.sum(lq2_ref[...] * lk2_ref[...], axis=-1, keepdims=True))
    return e1 - e2 + lam_init


def _softmax_step(s, v, m, l, acc):
    m_new = jnp.maximum(m, jnp.max(s, axis=-1, keepdims=True))
    alpha = jnp.exp2(m - m_new)
    p = jnp.exp2(s - m_new)
    l = alpha * l + jnp.sum(p, axis=-1, keepdims=True)
    acc = alpha * acc + _dot(p.astype(BF16), v)
    return m_new, l, acc


def _attn_kernel(lq1_ref, lk1_ref, lq2_ref, lk2_ref, q_ref, k_ref, vt_ref, g_ref, o_ref, s_s, a_s,
                 *, lam_init, tq, ks, unroll):
    seq = q_ref.shape[0]
    lane = lax.broadcasted_iota(jnp.int32, (tq, LANES), 1)
    rel = lax.broadcasted_iota(jnp.int32, (ks, tq), 0) - lax.broadcasted_iota(jnp.int32, (ks, tq), 1)
    lam = _lambda(lq1_ref, lk1_ref, lq2_ref, lk2_ref, lam_init)
    n_mask = max(1, tq // ks)

    def q_tile(i, _):
        q0 = pl.multiple_of(i * tq, tq)
        q = q_ref[pl.ds(q0, tq), :]
        zero = jnp.zeros_like(q)
        qs = (jnp.where(lane < DIFF_HEAD_DIM, q, zero), jnp.where(lane >= DIFF_HEAD_DIM, q, zero))
        a_s[...] = jnp.zeros_like(a_s)
        n_full = (q0 + 1) // ks

        def score(j, slot, lo):
            k = k_ref[pl.ds(pl.multiple_of(j * ks, ks), ks), :]
            for mp in range(2):
                s_s[slot, mp, :, lo:] = _dot_nt(k, qs[mp][lo:])

        def step(j, cur, ms, masked, lo=0, lo_next=0):
            if lo_next is not None:
                score(j + 1, 1 - cur, lo_next)
            k0 = pl.multiple_of(j * ks, ks)
            vt = vt_ref[0, 0, :, pl.ds(k0, ks)]
            out = []
            for mp in range(2):
                s = s_s[cur, mp, :, lo:]
                if masked:
                    s = jnp.where(rel[:, lo:] <= q0 - k0, s, -jnp.inf)
                m_old = ms[mp][:, lo:]
                m_new = jnp.maximum(m_old, jnp.max(s, axis=0, keepdims=True))
                p = jnp.exp2(s - m_new).astype(BF16)
                a_s[mp, :, lo:] = jnp.exp2(m_old - m_new) * a_s[mp, :, lo:] + _dot(vt, p)
                out.append(m_new if lo == 0 else jnp.concatenate([ms[mp][:, :lo], m_new], axis=1))
            return tuple(out)

        minf = jnp.full((1, tq), -jnp.inf, F32)
        score(0, 0, 0)

        def trip(jj, ms):
            for u in range(unroll):
                ms = step(unroll * jj + u, u % 2, ms, False)
            return ms

        ms = lax.fori_loop(0, n_full // unroll, trip, (minf, minf))
        for r in range(n_mask):
            lo_next = (r + 1) * ks if r + 1 < n_mask else None
            ms = step(n_full + r, r % 2, ms, True, r * ks, lo_next)
        a1, a2 = a_s[0], a_s[1]
        w = a1[:DIFF_V_DIM] / a1[DIFF_V_DIM:DIFF_V_DIM + 1] - lam * (a2[:DIFF_V_DIM] / a2[DIFF_V_DIM:DIFF_V_DIM + 1])
        o_ref[pl.ds(q0, tq), :] = (_rms(w.T, g_ref[...]) * (1.0 - lam_init)).astype(o_ref.dtype)
        return 0

    lax.fori_loop(0, seq // tq, q_tile, 0)


def _attn_prompt(lams, dqb, dkb, vt, diff_g, nb, seq, lam_init, tq=1024, ks=256, unroll=4):
    t = dqb.shape[0]
    assert unroll % 2 == 0 and tq % (unroll * ks) == 0 and seq % tq == 0
    lam_spec = pl.BlockSpec((1, DIFF_HEAD_DIM), lambda b, h: (0, 0))
    seqspec = pl.BlockSpec((seq, LANES), lambda b, h: (b, h))
    return pl.pallas_call(
        functools.partial(_attn_kernel, lam_init=lam_init, tq=tq, ks=ks, unroll=unroll),
        grid=(nb, DIFF_HEADS),
        in_specs=[lam_spec] * 4 + [seqspec, seqspec, pl.BlockSpec((1, 1, VT_ROWS, seq), lambda b, h: (b, h, 0, 0)),
                                  pl.BlockSpec((1, DIFF_V_DIM), lambda b, h: (0, 0))],
        out_specs=seqspec,
        out_shape=jax.ShapeDtypeStruct((t, DIFF_V), BF16),
        scratch_shapes=[pltpu.VMEM((2, 2, ks, tq), F32), pltpu.VMEM((2, VT_ROWS, tq), F32)],
        compiler_params=_params(("parallel", "parallel")),
        name="attn_prompt",
    )(*lams, dqb, dkb, vt, diff_g)


def _decode_kernel(pt_ref, lq1_ref, lk1_ref, lq2_ref, lk2_ref, q_ref, kn_ref, vn_ref, g_ref, ck_ref, cv_ref,
                   o_ref, kbuf, vbuf, sem, m_s, l_s, acc_s, *, lam_init, ppc, nch, nseq, page_rows):
    b = pl.program_id(0)
    c = pl.program_id(1)
    step = b * nch + c
    slot = lax.rem(step, DECODE_SLOTS)
    nrow = 2 * DIFF_HEADS

    def copies(n):
        bb, cc, sl = n // nch, lax.rem(n, nch), lax.rem(n, DECODE_SLOTS)
        out = []
        for p in range(ppc):
            page = pt_ref[bb, cc * ppc + p]
            dst = pl.ds(p * page_rows, page_rows)
            out.append(pltpu.make_async_copy(ck_ref.at[page], kbuf.at[sl, dst, :], sem.at[0, sl]))
            out.append(pltpu.make_async_copy(cv_ref.at[page], vbuf.at[sl, dst, :], sem.at[1, sl]))
        return out

    ahead = DECODE_SLOTS - 1

    @pl.when(step == 0)
    def _():
        for n in range(ahead):
            for cp in copies(jnp.int32(n)):
                cp.start()

    @pl.when(step + ahead < nseq * nch)
    def _():
        for cp in copies(step + ahead):
            cp.start()

    @pl.when(c == 0)
    def _():
        m_s[...] = jnp.full_like(m_s, -jnp.inf)
        l_s[...] = jnp.zeros_like(l_s)
        acc_s[...] = jnp.zeros_like(acc_s)

    for cp in copies(step):
        cp.wait()

    rowi = lax.broadcasted_iota(jnp.int32, (nrow, LANES), 0)
    lane = lax.broadcasted_iota(jnp.int32, (nrow, LANES), 1)
    q4 = q_ref[0].astype(F32)
    q8 = jnp.concatenate([q4, q4], axis=0)
    qm = jnp.where((rowi < DIFF_HEADS) == (lane < DIFF_HEAD_DIM), q8, 0.0)

    keys = ppc * page_rows
    s = _dot_nt(qm.astype(BF16), kbuf[slot].astype(BF16))
    col = lax.broadcasted_iota(jnp.int32, (nrow, keys), 1)
    rowk = lax.broadcasted_iota(jnp.int32, (nrow, keys), 0)
    s = jnp.where((col % DIFF_HEADS) == (rowk % DIFF_HEADS), s, -jnp.inf)
    m, l, acc = _softmax_step(s, vbuf[slot].astype(BF16), m_s[...], l_s[...], acc_s[...])
    m_s[...] = m
    l_s[...] = l
    acc_s[...] = acc

    @pl.when(c == nch - 1)
    def _():
        lam = _lambda(lq1_ref, lk1_ref, lq2_ref, lk2_ref, lam_init)
        kn = kn_ref[0].astype(BF16).astype(F32)
        vn = vn_ref[0].astype(BF16).astype(F32)
        sn = jnp.sum(qm * jnp.concatenate([kn, kn], axis=0), axis=-1, keepdims=True)
        m_new = jnp.maximum(m, sn)
        alpha = jnp.exp2(m - m_new)
        p = jnp.exp2(sn - m_new)
        lf = alpha * l + p
        w = (alpha * acc + p.astype(BF16).astype(F32) * jnp.concatenate([vn, vn], axis=0)) / lf
        out = w[:DIFF_HEADS] - lam * w[DIFF_HEADS:]
        o_ref[0] = _rms(out, g_ref[...]) * (1.0 - lam_init)


def _attn_decode(page_table, lams, dqb, dk, dv, diff_g, cache_k, cache_v, lam_init, ppc):
    n, n_pages = page_table.shape
    nch = n_pages // ppc
    n_pool, page = cache_k.shape[1], cache_k.shape[2]
    page_rows = page * DIFF_HEADS
    ck = cache_k.reshape(n_pool, page_rows, 2 * DIFF_HEAD_DIM)
    cv = cache_v.reshape(n_pool, page_rows, DIFF_V_DIM)
    lam_spec = pl.BlockSpec((1, DIFF_HEAD_DIM), lambda b, c, pt: (0, 0))
    rowspec = pl.BlockSpec((1, DIFF_HEADS, LANES), lambda b, c, pt: (b, 0, 0))
    anyspec = pl.BlockSpec(memory_space=pl.ANY)
    kern = functools.partial(_decode_kernel, lam_init=lam_init, ppc=ppc, nch=nch, nseq=n, page_rows=page_rows)
    nrow = 2 * DIFF_HEADS
    return pl.pallas_call(
        kern,
        grid_spec=pltpu.PrefetchScalarGridSpec(
            num_scalar_prefetch=1,
            grid=(n, nch),
            in_specs=[lam_spec] * 4 + [rowspec, rowspec, rowspec,
                                      pl.BlockSpec((1, DIFF_V_DIM), lambda b, c, pt: (0, 0)), anyspec, anyspec],
            out_specs=rowspec,
            scratch_shapes=[pltpu.VMEM((DECODE_SLOTS, ppc * page_rows, LANES), F32),
                            pltpu.VMEM((DECODE_SLOTS, ppc * page_rows, LANES), F32),
                            pltpu.SemaphoreType.DMA((2, DECODE_SLOTS)),
                            pltpu.VMEM((nrow, 1), F32), pltpu.VMEM((nrow, 1), F32),
                            pltpu.VMEM((nrow, DIFF_V_DIM), F32)]),
        out_shape=jax.ShapeDtypeStruct((n, DIFF_HEADS, LANES), F32),
        compiler_params=_params(("arbitrary", "arbitrary")),
        name="attn_decode",
    )(page_table, *lams, dqb.reshape(n, DIFF_HEADS, LANES), dk.reshape(n, DIFF_HEADS, LANES),
      dv.reshape(n, DIFF_HEADS, LANES), diff_g, ck, cv)


def _post_kernel(x_ref, go_ref, do_ref, wo_ref, g2_ref, wrh_ref, wrl_ref, br_ref, x1_ref, t_ref, gates_ref):
    x1 = x_ref[...] + _dot(go_ref[...].astype(BF16), wo_ref[:GLA_V, :]) + _dot(do_ref[...].astype(BF16), wo_ref[GLA_V:, :])
    x1_ref[...] = x1
    t = _rms(x1, g2_ref[...])
    t_ref[...] = t.astype(BF16)
    t_hi = t.astype(BF16)
    t_lo = (t - t_hi.astype(F32)).astype(BF16)
    w_hi = wrh_ref[...]
    w_lo = wrl_ref[...]
    logits = (_dot(t_lo, w_hi) + _dot(t_hi, w_lo)) + _dot(t_hi, w_hi) + br_ref[...]
    lane = lax.broadcasted_iota(jnp.int32, logits.shape, 1)
    neg = -jnp.inf
    is_group = (lane >= N_EXPERTS) & (lane < N_EXPERTS + N_GROUPS)
    lg = jnp.where(is_group, logits, neg)
    lg_max = jnp.max(lg, axis=-1, keepdims=True)
    g_lane = jnp.min(jnp.where(lg == lg_max, lane, LANES), axis=-1, keepdims=True)
    p_top = 1.0 / jnp.sum(jnp.exp(lg - lg_max), axis=-1, keepdims=True)
    in_g = (lane // EXPERTS_PER_GROUP) == (g_lane - N_EXPERTS)
    le = jnp.where(in_g, logits, neg)
    v1 = jnp.max(le, axis=-1, keepdims=True)
    i1 = jnp.min(jnp.where(le == v1, lane, LANES), axis=-1, keepdims=True)
    le2 = jnp.where(lane == i1, neg, le)
    v2 = jnp.max(le2, axis=-1, keepdims=True)
    i2 = jnp.min(jnp.where(le2 == v2, lane, LANES), axis=-1, keepdims=True)
    e2 = jnp.exp(v2 - v1)
    w1 = p_top / (1.0 + e2)
    w2 = p_top * e2 / (1.0 + e2)
    gates_ref[...] = jnp.where(lane == i1, w1, 0.0) + jnp.where(lane == i2, w2, 0.0)


def _post(x, gla_o, diff_o, wo, g2, wr_hi, wr_lo, br, tm):
    t = x.shape[0]
    row = lambda w: pl.BlockSpec((tm, w), lambda i: (i, 0))
    full = lambda a: pl.BlockSpec(a.shape, lambda i: (0,) * a.ndim)
    return pl.pallas_call(
        _post_kernel,
        grid=(t // tm,),
        in_specs=[row(D_MODEL), row(GLA_V), row(DIFF_V), full(wo), full(g2), full(wr_hi), full(wr_lo), full(br)],
        out_specs=[row(D_MODEL), row(D_MODEL), row(LANES)],
        out_shape=[jax.ShapeDtypeStruct((t, D_MODEL), F32), jax.ShapeDtypeStruct((t, D_MODEL), BF16),
                   jax.ShapeDtypeStruct((t, LANES), F32)],
        compiler_params=_params(("parallel",)),
        name="post",
    )(x, gla_o, diff_o, wo, g2, wr_hi, wr_lo, br)


def _moe_kernel(t_ref, gates_ref, x1_ref, wgu_ref, wd_ref, gf_ref, y_ref, *, eps_per_step):
    e0 = pl.program_id(1)

    @pl.when(e0 == 0)
    def _():
        y_ref[...] = jnp.zeros_like(y_ref)

    t = t_ref[...]
    gates = gates_ref[...]
    ei = lax.broadcasted_iota(jnp.int32, gates.shape, 1)
    acc = y_ref[...]
    for k in range(eps_per_step):
        gu = _dot(t, wgu_ref[k])
        gate = jnp.sum(jnp.where(ei == e0 * eps_per_step + k, gates, 0.0), axis=-1, keepdims=True)
        hid = _silu(gu[:, :EXPERT_FF]) * gu[:, EXPERT_FF:] * gate
        acc = acc + _dot(hid.astype(BF16), wd_ref[k])
    y_ref[...] = acc

    @pl.when(e0 == pl.num_programs(1) - 1)
    def _():
        y_ref[...] = _rms(x1_ref[...] + y_ref[...], gf_ref[...])


def _moe(t, gates, x1, wgu, wd, gf, tm, eps_per_step=2 * EXPERTS_PER_GROUP):
    n = t.shape[0]
    row = lambda w: pl.BlockSpec((tm, w), lambda i, e: (i, 0))
    return pl.pallas_call(
        functools.partial(_moe_kernel, eps_per_step=eps_per_step),
        grid=(n // tm, N_EXPERTS // eps_per_step),
        in_specs=[row(D_MODEL), row(LANES), row(D_MODEL),
                  pl.BlockSpec((eps_per_step, D_MODEL, 2 * EXPERT_FF), lambda i, e: (e, 0, 0)),
                  pl.BlockSpec((eps_per_step, EXPERT_FF, D_MODEL), lambda i, e: (e, 0, 0)),
                  pl.BlockSpec((1, D_MODEL), lambda i, e: (0, 0))],
        out_specs=row(D_MODEL),
        out_shape=jax.ShapeDtypeStruct((n, D_MODEL), F32),
        compiler_params=_params(("parallel", "arbitrary")),
        name="moe",
    )(t, gates, x1, wgu, wd, gf)


def kernel(x_prompt, x_sample, cache_k, cache_v, state_gla, page_table, norm1_g, w_in, w_a2, b_a, gla_norm_g,
           lambda_q1, lambda_k1, lambda_q2, lambda_k2, diff_norm_g, w_out, norm2_g, w_router_g, b_router_g,
           w_router_e, b_router_e, w_gate, w_up, w_down, norm_f_g):
    nb, seq, _ = x_prompt.shape
    ns = x_sample.shape[0]
    depth = w_in.shape[0]
    assert depth == 1 and x_sample.shape[1] == 1
    lam_init = 0.8 - 0.6 * math.exp(-0.3 * 0)

    c_code = 2 * GLA_QK + 2 * GLA_V
    w = w_in[0]
    wm = jnp.concatenate([w[:, :c_code], w[:, c_code + GLA_RANK:]], axis=1).astype(BF16)
    wc = jnp.pad(w[:, c_code:c_code + GLA_RANK], ((0, 0), (0, LANES - GLA_RANK))).astype(BF16)
    wa2p = jnp.pad(w_a2[0], ((0, LANES - GLA_RANK), (0, 0))).astype(BF16)
    ba = b_a[0].reshape(1, GLA_QK)
    g1 = norm1_g[0].reshape(1, D_MODEL)
    g2 = norm2_g[0].reshape(1, D_MODEL)
    gf = norm_f_g.reshape(1, D_MODEL)
    gla_g = gla_norm_g[0].reshape(1, GLA_DV)
    diff_g = diff_norm_g[0].reshape(1, DIFF_V_DIM)
    lams = [a[0].reshape(1, DIFF_HEAD_DIM) for a in (lambda_q1, lambda_k1, lambda_q2, lambda_k2)]
    wo = w_out[0].astype(BF16)
    wr = jnp.concatenate([w_router_e[0].transpose(1, 0, 2).reshape(D_MODEL, N_EXPERTS), w_router_g[0]], axis=1)
    wr = jnp.pad(wr, ((0, 0), (0, LANES - N_GROUPS - N_EXPERTS)))
    wr_hi = wr.astype(BF16)
    wr_lo = (wr - wr_hi.astype(F32)).astype(BF16)
    br = jnp.pad(jnp.concatenate([b_router_e[0].reshape(N_EXPERTS), b_router_g[0]]),
                 (0, LANES - N_GROUPS - N_EXPERTS)).reshape(1, LANES)
    wgu = jnp.concatenate([w_gate[0], w_up[0]], axis=-1).astype(BF16)
    wd = w_down[0].astype(BF16)

    def dense_tail(x, gla_o, diff_o, tm_post, tm_moe):
        x1, t, gates = _post(x, gla_o, diff_o, wo, g2, wr_hi, wr_lo, br, tm_post)
        return _moe(t, gates, x1, wgu, wd, gf, tm_moe)

    xp = x_prompt.reshape(nb * seq, D_MODEL)
    gq, gk, la, gv, gr, dqb, dk, dv, dkb, vt = _inproj(xp, g1, wm, wc, wa2p, ba, 512, (nb, seq))
    s0 = jnp.zeros((nb, GLA_HEADS, GLA_DK, GLA_DV), F32)
    gla_o, s_prompt = _gla_prompt(gq, gk, la, gv, gr, gla_g, s0, nb, 256)
    diff_o = _attn_prompt(lams, dqb, dkb, vt, diff_g, nb, seq, lam_init)
    y_prompt = dense_tail(xp, gla_o, diff_o, 512, 1024).reshape(nb, seq, D_MODEL)

    xs = x_sample.reshape(ns, D_MODEL)
    sgq, sgk, sla, sgv, sgr, sdqb, sdk, sdv = _inproj(xs, g1, wm, wc, wa2p, ba, ns)
    sgla_o, s_sample = _gla_step(sgq, sgk, sla, sgv, sgr, gla_g, state_gla[0])
    sdiff_o = _attn_decode(page_table, lams, sdqb, sdk, sdv, diff_g, cache_k, cache_v, lam_init, 8)
    y_sample = dense_tail(xs, sgla_o.reshape(ns, GLA_V), sdiff_o.reshape(ns, DIFF_V), ns, ns).reshape(ns, 1, D_MODEL)

    k_prompt = dk.reshape(1, nb, seq, DIFF_HEADS, 2 * DIFF_HEAD_DIM)
    v_prompt = dv.reshape(1, nb, seq, DIFF_HEADS, DIFF_V_DIM)
    k_sample = sdk.reshape(1, ns, 1, DIFF_HEADS, 2 * DIFF_HEAD_DIM)
    v_sample = sdv.reshape(1, ns, 1, DIFF_HEADS, DIFF_V_DIM)
    return (y_prompt, y_sample, k_prompt, v_prompt, s_prompt[None], k_sample, v_sample, s_sample[None])
```

```python
import functools
import math

import jax
import jax.numpy as jnp
from jax import lax
from jax.experimental import pallas as pl
from jax.experimental.pallas import tpu as pltpu

F32 = jnp.float32
BF16 = jnp.bfloat16

D_MODEL = 1024
GLA_HEADS = 4
GLA_DK = 64
GLA_DV = 128
GLA_RANK = 16
GLA_TAU = 16.0
GLA_CHUNK = 16
DIFF_HEADS = 4
DIFF_HEAD_DIM = 64
DIFF_V_DIM = 128
GLA_QK = GLA_HEADS * GLA_DK
GLA_V = GLA_HEADS * GLA_DV
DIFF_QK = DIFF_HEADS * 2 * DIFF_HEAD_DIM
DIFF_V = DIFF_HEADS * DIFF_V_DIM
N_GROUPS = 4
EXPERTS_PER_GROUP = 4
N_EXPERTS = 16
EXPERT_FF = 256
EPS = 1e-6
LANES = 128
BF16_SUBLANES = 16
LOG2E = math.log2(math.e)
DECODE_SLOTS = 3
VT_ROWS = DIFF_V_DIM + BF16_SUBLANES
VMEM_LIMIT = 56 * 1024 * 1024

_C_GQ, _C_GK, _C_GV, _C_GR, _C_DQ, _C_DK, _C_DV, _C_END = 0, 256, 512, 1024, 1536, 2048, 2560, 3072


def _rms(x, g):
    return x * lax.rsqrt(jnp.mean(x * x, axis=-1, keepdims=True) + EPS) * g


def _silu(x):
    return x * (1.0 / (1.0 + jnp.exp(-x)))


def _log_sigmoid(x):
    return jnp.minimum(x, 0.0) - jnp.log1p(jnp.exp(-jnp.abs(x)))


def _dot(a, b):
    return jnp.dot(a, b, preferred_element_type=F32)


def _dot_nt(a, b):
    return lax.dot_general(a, b, (((1,), (1,)), ((), ())), preferred_element_type=F32)


def _split3(x):
    hi = x.astype(BF16)
    r = x - hi.astype(F32)
    mid = r.astype(BF16)
    lo = (r - mid.astype(F32)).astype(BF16)
    return hi, mid, lo


def _params(sem):
    return pltpu.CompilerParams(dimension_semantics=sem, vmem_limit_bytes=VMEM_LIMIT)


def _inproj_kernel(x_ref, g_ref, wm_ref, wc_ref, wa2_ref, ba_ref,
                   gq_ref, gk_ref, la_ref, gv_ref, gr_ref, dqb_ref, dk_ref, dv_ref, *prompt_refs):
    h = _rms(x_ref[...], g_ref[...]).astype(BF16)

    def seg(lo, hi):
        return _dot(h, wm_ref[:, lo:hi])

    gq_ref[...] = seg(_C_GQ, _C_GK) * (GLA_DK ** -0.5)
    gk_ref[...] = seg(_C_GK, _C_GV)
    gv_ref[...] = seg(_C_GV, _C_GR)
    gr_ref[...] = seg(_C_GR, _C_DQ)
    dqb_ref[...] = (seg(_C_DQ, _C_DK) * (DIFF_HEAD_DIM ** -0.5 * LOG2E)).astype(BF16)
    dk = seg(_C_DK, _C_DV)
    dv = seg(_C_DV, _C_END)
    tm = dk.shape[0]
    for hd in range(DIFF_HEADS):
        rows = pl.ds(hd, tm, stride=DIFF_HEADS)
        dk_ref[rows, :] = dk[:, hd * LANES:(hd + 1) * LANES]
        dv_ref[rows, :] = dv[:, hd * LANES:(hd + 1) * LANES]
    if prompt_refs:
        dkb_ref, vt_ref = prompt_refs
        dkb_ref[...] = dk.astype(BF16)
        ones_row = (lax.broadcasted_iota(jnp.int32, (VT_ROWS - DIFF_V_DIM, tm), 0) == 0).astype(F32)
        for hd in range(DIFF_HEADS):
            vt_ref[0, hd, :DIFF_V_DIM, :] = dv[:, hd * DIFF_V_DIM:(hd + 1) * DIFF_V_DIM].T.astype(BF16)
            vt_ref[0, hd, DIFF_V_DIM:, :] = ones_row.astype(BF16)
    code = _dot(h, wc_ref[...])
    pre = _dot(code.astype(BF16), wa2_ref[...]) + ba_ref[...]
    la_ref[...] = _log_sigmoid(pre) * (1.0 / GLA_TAU)


def _inproj(x, g1, wm, wc, wa2p, ba, tm, prompt_shape=None):
    t = x.shape[0]
    row = lambda w: pl.BlockSpec((tm, w), lambda i: (i, 0))
    full = lambda a: pl.BlockSpec(a.shape, lambda i: (0,) * a.ndim)
    outs = [(GLA_QK, F32), (GLA_QK, F32), (GLA_QK, F32), (GLA_V, F32), (GLA_V, F32), (DIFF_QK, BF16)]
    out_specs = [row(w) for w, _ in outs]
    out_shape = [jax.ShapeDtypeStruct((t, w), dt) for w, dt in outs]
    out_specs += [pl.BlockSpec((tm * DIFF_HEADS, LANES), lambda i: (i, 0))] * 2
    out_shape += [jax.ShapeDtypeStruct((t * DIFF_HEADS, LANES), F32)] * 2
    if prompt_shape is not None:
        nb, seq = prompt_shape
        nl = seq // tm
        out_specs += [row(DIFF_QK), pl.BlockSpec((1, DIFF_HEADS, VT_ROWS, tm), lambda i: (i // nl, 0, 0, i % nl))]
        out_shape += [jax.ShapeDtypeStruct((t, DIFF_QK), BF16),
                      jax.ShapeDtypeStruct((nb, DIFF_HEADS, VT_ROWS, seq), BF16)]
    return pl.pallas_call(
        _inproj_kernel,
        grid=(t // tm,),
        in_specs=[row(D_MODEL), full(g1), full(wm), full(wc), full(wa2p), full(ba)],
        out_specs=out_specs,
        out_shape=out_shape,
        compiler_params=_params(("parallel",)),
        name="inproj",
    )(x, g1, wm, wc, wa2p, ba)


def _dot_tn(a, b):
    return lax.dot_general(a, b, (((0,), (0,)), ((), ())), preferred_element_type=F32)


def _gla_kernel(gq_ref, gk_ref, la_ref, gv_ref, gr_ref, tri_ref, last_ref, csel_ref, ind_ref, rep_ref, same_ref, g_ref, s0_ref,
                o_ref, sfin_ref,
                s_s, b_s, qi_s, kd_s, adec_s, o_s, p_s):
    l = pl.program_id(1)
    tl = gq_ref.shape[0]
    c16 = GLA_CHUNK

    @pl.when(l == 0)
    def _():
        for h in range(GLA_HEADS):
            s_s[h * GLA_DK:(h + 1) * GLA_DK, :] = s0_ref[0, h]

    hi, mid, lo = _split3(la_ref[...])

    def cs(m):
        return (_dot(m, lo) + _dot(m, mid)) + _dot(m, hi)

    b = cs(tri_ref[...])
    bl = cs(last_ref[...])
    b_s[...] = b
    qi_s[...] = gq_ref[...] * jnp.exp(b)
    kd_s[...] = (gk_ref[...] * jnp.exp(bl - b)).astype(BF16)
    csel = csel_ref[...]
    adec_s[...] = jnp.exp((_dot_tn(lo, csel) + _dot_tn(mid, csel)) + _dot_tn(hi, csel))

    rowi = lax.broadcasted_iota(jnp.int32, (c16, GLA_QK), 0)
    vb = gv_ref[...].astype(BF16)

    for c in range(tl // c16):
        r0 = c * c16
        q_c = gq_ref[r0:r0 + c16, :]
        b_c = b[r0:r0 + c16, :]
        for j in range(c16):
            bj = b_s[r0 + j:r0 + j + 1, :]
            kj = gk_ref[r0 + j:r0 + j + 1, :]
            e = jnp.exp(jnp.where(rowi >= j, b_c - bj, -jnp.inf))
            p_s[r0:r0 + c16, j * GLA_QK:(j + 1) * GLA_QK] = (q_c * e * kj).astype(BF16)
    scores = _dot(p_s[...], ind_ref[...]).astype(BF16)
    a_bd = _dot(scores, rep_ref[...]).astype(BF16) * same_ref[...]
    for h in range(GLA_HEADS):
        dv = slice(h * GLA_DV, (h + 1) * GLA_DV)
        o_s[:, dv] = _dot(a_bd[:, h * tl:(h + 1) * tl], vb[:, dv])

    adec = adec_s[...]
    state = [s_s[h * GLA_DK:(h + 1) * GLA_DK, :] for h in range(GLA_HEADS)]
    for c in range(tl // c16):
        rows = slice(c * c16, (c + 1) * c16)
        qi_c = qi_s[rows, :].astype(BF16)
        kd_c = kd_s[rows, :]
        vb_c = vb[rows, :]
        a_col = adec[:, c:c + 1]
        o_parts = []
        for h in range(GLA_HEADS):
            dk = slice(h * GLA_DK, (h + 1) * GLA_DK)
            dv = slice(h * GLA_DV, (h + 1) * GLA_DV)
            o_parts.append(_dot(qi_c[:, dk], state[h].astype(BF16)))
            state[h] = a_col[dk] * state[h] + _dot_tn(kd_c[:, dk], vb_c[:, dv])
        o_s[rows, :] += jnp.concatenate(o_parts, axis=1)
    for h in range(GLA_HEADS):
        s_s[h * GLA_DK:(h + 1) * GLA_DK, :] = state[h]

    o = o_s[...]
    for h in range(GLA_HEADS):
        sl = slice(h * GLA_DV, (h + 1) * GLA_DV)
        o_ref[:, sl] = (_rms(o[:, sl], g_ref[...]) * _silu(gr_ref[:, sl])).astype(o_ref.dtype)

    @pl.when(l == pl.num_programs(1) - 1)
    def _():
        for h in range(GLA_HEADS):
            sfin_ref[0, h] = s_s[h * GLA_DK:(h + 1) * GLA_DK, :]


def _gla_prompt(gq, gk, la, gv, gr, gla_g, s0, nb, tl):
    t = gq.shape[0]
    nl = t // nb // tl
    ti = jnp.arange(tl)
    same = (ti[:, None] // GLA_CHUNK) == (ti[None, :] // GLA_CHUNK)
    tri = (same & (ti[None, :] <= ti[:, None])).astype(BF16)
    last = same.astype(BF16)
    csel = ((ti[:, None] // GLA_CHUNK) == jnp.arange(LANES)[None, :]).astype(BF16)
    kj = jnp.arange(GLA_CHUNK * GLA_QK)
    ln = jnp.arange(LANES)
    ind = ((kj[:, None] // GLA_QK == ln[None, :] % GLA_CHUNK)
           & ((kj[:, None] % GLA_QK) // GLA_DK == ln[None, :] // GLA_CHUNK)).astype(BF16)
    ks = jnp.arange(GLA_HEADS * tl)
    rep = ((ln[:, None] // GLA_CHUNK == ks[None, :] // tl)
           & (ln[:, None] % GLA_CHUNK == ks[None, :] % GLA_CHUNK)).astype(BF16)
    same_chunk = (ti[:, None] // GLA_CHUNK == (ks[None, :] % tl) // GLA_CHUNK).astype(BF16)
    row = lambda w: pl.BlockSpec((tl, w), lambda b, l: (b * nl + l, 0))
    full = lambda shp: pl.BlockSpec(shp, lambda b, l: (0,) * len(shp))
    st = pl.BlockSpec((1, GLA_HEADS, GLA_DK, GLA_DV), lambda b, l: (b, 0, 0, 0))
    return pl.pallas_call(
        _gla_kernel,
        grid=(nb, nl),
        in_specs=[row(GLA_QK), row(GLA_QK), row(GLA_QK), row(GLA_V), row(GLA_V),
                  full((tl, tl)), full((tl, tl)), full((tl, LANES)), full(ind.shape), full(rep.shape), full(same_chunk.shape),
                  full((1, GLA_DV)), st],
        out_specs=[row(GLA_V), st],
        out_shape=[jax.ShapeDtypeStruct((t, GLA_V), BF16),
                   jax.ShapeDtypeStruct((nb, GLA_HEADS, GLA_DK, GLA_DV), F32)],
        scratch_shapes=[pltpu.VMEM((GLA_QK, GLA_DV), F32), pltpu.VMEM((tl, GLA_QK), F32),
                        pltpu.VMEM((tl, GLA_QK), F32), pltpu.VMEM((tl, GLA_QK), BF16),
                        pltpu.VMEM((GLA_QK, LANES), F32),
                        pltpu.VMEM((tl, GLA_V), F32), pltpu.VMEM((tl, GLA_CHUNK * GLA_QK), BF16)],
        compiler_params=_params(("parallel", "arbitrary")),
        name="gla_prompt",
    )(gq, gk, la, gv, gr, tri, last, csel, ind, rep, same_chunk, gla_g, s0)


def _gla_step_kernel(q_ref, k_ref, la_ref, v_ref, gr_ref, g_ref, s_ref, o_ref, snew_ref):
    eye = (lax.broadcasted_iota(jnp.int32, (GLA_QK, GLA_QK), 0)
           == lax.broadcasted_iota(jnp.int32, (GLA_QK, GLA_QK), 1))

    def column(row):
        return jnp.sum(jnp.where(eye, row, 0.0), axis=1, keepdims=True)

    for i in range(q_ref.shape[0]):
        q_col, k_col, a_col = column(q_ref[i]), column(k_ref[i]), jnp.exp(column(la_ref[i]))
        for h in range(GLA_HEADS):
            dk = slice(h * GLA_DK, (h + 1) * GLA_DK)
            dv = slice(h * GLA_DV, (h + 1) * GLA_DV)
            s_new = a_col[dk] * s_ref[i, h] + k_col[dk] * v_ref[i, :, dv]
            snew_ref[i, h] = s_new
            o = jnp.sum(q_col[dk] * s_new, axis=0, keepdims=True)
            o_ref[i, :, dv] = _rms(o, g_ref[...]) * _silu(gr_ref[i, :, dv])


def _gla_step(gq, gk, la, gv, gr, gla_g, state, per_step=8):
    n = gq.shape[0]
    assert n % per_step == 0
    col = pl.BlockSpec((per_step, 1, GLA_QK), lambda b: (b, 0, 0))
    rowv = pl.BlockSpec((per_step, 1, GLA_V), lambda b: (b, 0, 0))
    st = pl.BlockSpec((per_step, GLA_HEADS, GLA_DK, GLA_DV), lambda b: (b, 0, 0, 0))
    return pl.pallas_call(
        _gla_step_kernel,
        grid=(n // per_step,),
        in_specs=[col, col, col, rowv, rowv, pl.BlockSpec((1, GLA_DV), lambda b: (0, 0)), st],
        out_specs=[rowv, st],
        out_shape=[jax.ShapeDtypeStruct((n, 1, GLA_V), F32),
                   jax.ShapeDtypeStruct((n, GLA_HEADS, GLA_DK, GLA_DV), F32)],
        compiler_params=_params(("parallel",)),
        name="gla_step",
    )(gq.reshape(n, 1, GLA_QK), gk.reshape(n, 1, GLA_QK), la.reshape(n, 1, GLA_QK),
      gv.reshape(n, 1, GLA_V), gr.reshape(n, 1, GLA_V), gla_g, state)


def _lambda(lq1_ref, lk1_ref, lq2_ref, lk2_ref, lam_init):
    e1 = jnp.exp(jnp.sum(lq1_ref[...] * lk1_ref[...], axis=-1, keepdims=True))
    e2 = jnp.exp(jnp.sum(lq2_ref[...] * lk2_ref[...], axis=-1, keepdims=True))
    return e1 - e2 + lam_init


def _softmax_step(s, v, m, l, acc):
    m_new = jnp.maximum(m, jnp.max(s, axis=-1, keepdims=True))
    alpha = jnp.exp2(m - m_new)
    p = jnp.exp2(s - m_new)
    l = alpha * l + jnp.sum(p, axis=-1, keepdims=True)
    acc = alpha * acc + _dot(p.astype(BF16), v)
    return m_new, l, acc


def _attn_kernel(lq1_ref, lk1_ref, lq2_ref, lk2_ref, q_ref, k_ref, vt_ref, g_ref, o_ref, s_s, a_s,
                 *, lam_init, tq, ks, unroll):
    seq = q_ref.shape[0]
    lane = lax.broadcasted_iota(jnp.int32, (tq, LANES), 1)
    rel = lax.broadcasted_iota(jnp.int32, (ks, tq), 0) - lax.broadcasted_iota(jnp.int32, (ks, tq), 1)
    lam = _lambda(lq1_ref, lk1_ref, lq2_ref, lk2_ref, lam_init)
    n_mask = max(1, tq // ks)

    def q_tile(i, _):
        q0 = pl.multiple_of(i * tq, tq)
        q = q_ref[pl.ds(q0, tq), :]
        zero = jnp.zeros_like(q)
        qs = (jnp.where(lane < DIFF_HEAD_DIM, q, zero), jnp.where(lane >= DIFF_HEAD_DIM, q, zero))
        a_s[...] = jnp.zeros_like(a_s)
        n_full = (q0 + 1) // ks

        def score(j, slot, lo):
            k = k_ref[pl.ds(pl.multiple_of(j * ks, ks), ks), :]
            for mp in range(2):
                s_s[slot, mp, :, lo:] = _dot_nt(k, qs[mp][lo:])

        def step(j, cur, ms, masked, lo=0, lo_next=0):
            if lo_next is not None:
                score(j + 1, 1 - cur, lo_next)
            k0 = pl.multiple_of(j * ks, ks)
            vt = vt_ref[0, 0, :, pl.ds(k0, ks)]
            out = []
            for mp in range(2):
                s = s_s[cur, mp, :, lo:]
                if masked:
                    s = jnp.where(rel[:, lo:] <= q0 - k0, s, -jnp.inf)
                m_old = ms[mp][:, lo:]
                m_new = jnp.maximum(m_old, jnp.max(s, axis=0, keepdims=True))
                p = jnp.exp2(s - m_new).astype(BF16)
                a_s[mp, :, lo:] = jnp.exp2(m_old - m_new) * a_s[mp, :, lo:] + _dot(vt, p)
                out.append(m_new if lo == 0 else jnp.concatenate([ms[mp][:, :lo], m_new], axis=1))
            return tuple(out)

        minf = jnp.full((1, tq), -jnp.inf, F32)
        score(0, 0, 0)

        def trip(jj, ms):
            for u in range(unroll):
                ms = step(unroll * jj + u, u % 2, ms, False)
            return ms

        ms = lax.fori_loop(0, n_full // unroll, trip, (minf, minf))
        for r in range(n_mask):
            lo_next = (r + 1) * ks if r + 1 < n_mask else None
            ms = step(n_full + r, r % 2, ms, True, r * ks, lo_next)
        a1, a2 = a_s[0], a_s[1]
        w = a1[:DIFF_V_DIM] / a1[DIFF_V_DIM:DIFF_V_DIM + 1] - lam * (a2[:DIFF_V_DIM] / a2[DIFF_V_DIM:DIFF_V_DIM + 1])
        o_ref[pl.ds(q0, tq), :] = (_rms(w.T, g_ref[...]) * (1.0 - lam_init)).astype(o_ref.dtype)
        return 0

    lax.fori_loop(0, seq // tq, q_tile, 0)


def _attn_prompt(lams, dqb, dkb, vt, diff_g, nb, seq, lam_init, tq=1024, ks=256, unroll=4):
    t = dqb.shape[0]
    assert unroll % 2 == 0 and tq % (unroll * ks) == 0 and seq % tq == 0
    lam_spec = pl.BlockSpec((1, DIFF_HEAD_DIM), lambda b, h: (0, 0))
    seqspec = pl.BlockSpec((seq, LANES), lambda b, h: (b, h))
    return pl.pallas_call(
        functools.partial(_attn_kernel, lam_init=lam_init, tq=tq, ks=ks, unroll=unroll),
        grid=(nb, DIFF_HEADS),
        in_specs=[lam_spec] * 4 + [seqspec, seqspec, pl.BlockSpec((1, 1, VT_ROWS, seq), lambda b, h: (b, h, 0, 0)),
                                  pl.BlockSpec((1, DIFF_V_DIM), lambda b, h: (0, 0))],
        out_specs=seqspec,
        out_shape=jax.ShapeDtypeStruct((t, DIFF_V), BF16),
        scratch_shapes=[pltpu.VMEM((2, 2, ks, tq), F32), pltpu.VMEM((2, VT_ROWS, tq), F32)],
        compiler_params=_params(("parallel", "parallel")),
        name="attn_prompt",
    )(*lams, dqb, dkb, vt, diff_g)


def _decode_kernel(pt_ref, lq1_ref, lk1_ref, lq2_ref, lk2_ref, q_ref, kn_ref, vn_ref, g_ref, ck_ref, cv_ref,
                   o_ref, kbuf, vbuf, sem, m_s, l_s, acc_s, *, lam_init, ppc, nch, nseq, page_rows):
    b = pl.program_id(0)
    c = pl.program_id(1)
    step = b * nch + c
    slot = lax.rem(step, DECODE_SLOTS)
    nrow = 2 * DIFF_HEADS

    def copies(n):
        bb, cc, sl = n // nch, lax.rem(n, nch), lax.rem(n, DECODE_SLOTS)
        out = []
        for p in range(ppc):
            page = pt_ref[bb, cc * ppc + p]
            dst = pl.ds(p * page_rows, page_rows)
            out.append(pltpu.make_async_copy(ck_ref.at[page], kbuf.at[sl, dst, :], sem.at[0, sl]))
            out.append(pltpu.make_async_copy(cv_ref.at[page], vbuf.at[sl, dst, :], sem.at[1, sl]))
        return out

    ahead = DECODE_SLOTS - 1

    @pl.when(step == 0)
    def _():
        for n in range(ahead):
            for cp in copies(jnp.int32(n)):
                cp.start()

    @pl.when(step + ahead < nseq * nch)
    def _():
        for cp in copies(step + ahead):
            cp.start()

    @pl.when(c == 0)
    def _():
        m_s[...] = jnp.full_like(m_s, -jnp.inf)
        l_s[...] = jnp.zeros_like(l_s)
        acc_s[...] = jnp.zeros_like(acc_s)

    for cp in copies(step):
        cp.wait()

    rowi = lax.broadcasted_iota(jnp.int32, (nrow, LANES), 0)
    lane = lax.broadcasted_iota(jnp.int32, (nrow, LANES), 1)
    q4 = q_ref[0].astype(F32)
    q8 = jnp.concatenate([q4, q4], axis=0)
    qm = jnp.where((rowi < DIFF_HEADS) == (lane < DIFF_HEAD_DIM), q8, 0.0)

    keys = ppc * page_rows
    s = _dot_nt(qm.astype(BF16), kbuf[slot].astype(BF16))
    col = lax.broadcasted_iota(jnp.int32, (nrow, keys), 1)
    rowk = lax.broadcasted_iota(jnp.int32, (nrow, keys), 0)
    s = jnp.where((col % DIFF_HEADS) == (rowk % DIFF_HEADS), s, -jnp.inf)
    m, l, acc = _softmax_step(s, vbuf[slot].astype(BF16), m_s[...], l_s[...], acc_s[...])
    m_s[...] = m
    l_s[...] = l
    acc_s[...] = acc

    @pl.when(c == nch - 1)
    def _():
        lam = _lambda(lq1_ref, lk1_ref, lq2_ref, lk2_ref, lam_init)
        kn = kn_ref[0].astype(BF16).astype(F32)
        vn = vn_ref[0].astype(BF16).astype(F32)
        sn = jnp.sum(qm * jnp.concatenate([kn, kn], axis=0), axis=-1, keepdims=True)
        m_new = jnp.maximum(m, sn)
        alpha = jnp.exp2(m - m_new)
        p = jnp.exp2(sn - m_new)
        lf = alpha * l + p
        w = (alpha * acc + p.astype(BF16).astype(F32) * jnp.concatenate([vn, vn], axis=0)) / lf
        out = w[:DIFF_HEADS] - lam * w[DIFF_HEADS:]
        o_ref[0] = _rms(out, g_ref[...]) * (1.0 - lam_init)


def _attn_decode(page_table, lams, dqb, dk, dv, diff_g, cache_k, cache_v, lam_init, ppc):
    n, n_pages = page_table.shape
    nch = n_pages // ppc
    n_pool, page = cache_k.shape[1], cache_k.shape[2]
    page_rows = page * DIFF_HEADS
    ck = cache_k.reshape(n_pool, page_rows, 2 * DIFF_HEAD_DIM)
    cv = cache_v.reshape(n_pool, page_rows, DIFF_V_DIM)
    lam_spec = pl.BlockSpec((1, DIFF_HEAD_DIM), lambda b, c, pt: (0, 0))
    rowspec = pl.BlockSpec((1, DIFF_HEADS, LANES), lambda b, c, pt: (b, 0, 0))
    anyspec = pl.BlockSpec(memory_space=pl.ANY)
    kern = functools.partial(_decode_kernel, lam_init=lam_init, ppc=ppc, nch=nch, nseq=n, page_rows=page_rows)
    nrow = 2 * DIFF_HEADS
    return pl.pallas_call(
        kern,
        grid_spec=pltpu.PrefetchScalarGridSpec(
            num_scalar_prefetch=1,
            grid=(n, nch),
            in_specs=[lam_spec] * 4 + [rowspec, rowspec, rowspec,
                                      pl.BlockSpec((1, DIFF_V_DIM), lambda b, c, pt: (0, 0)), anyspec, anyspec],
            out_specs=rowspec,
            scratch_shapes=[pltpu.VMEM((DECODE_SLOTS, ppc * page_rows, LANES), F32),
                            pltpu.VMEM((DECODE_SLOTS, ppc * page_rows, LANES), F32),
                            pltpu.SemaphoreType.DMA((2, DECODE_SLOTS)),
                            pltpu.VMEM((nrow, 1), F32), pltpu.VMEM((nrow, 1), F32),
                            pltpu.VMEM((nrow, DIFF_V_DIM), F32)]),
        out_shape=jax.ShapeDtypeStruct((n, DIFF_HEADS, LANES), F32),
        compiler_params=_params(("arbitrary", "arbitrary")),
        name="attn_decode",
    )(page_table, *lams, dqb.reshape(n, DIFF_HEADS, LANES), dk.reshape(n, DIFF_HEADS, LANES),
      dv.reshape(n, DIFF_HEADS, LANES), diff_g, ck, cv)


def _post_kernel(x_ref, go_ref, do_ref, wo_ref, g2_ref, wr_ref, br_ref, x1_ref, t_ref, gates_ref):
    x1 = x_ref[...] + _dot(go_ref[...].astype(BF16), wo_ref[:GLA_V, :]) + _dot(do_ref[...].astype(BF16), wo_ref[GLA_V:, :])
    x1_ref[...] = x1
    t = _rms(x1, g2_ref[...])
    t_ref[...] = t.astype(BF16)
    t_hi = t.astype(BF16)
    t_lo = (t - t_hi.astype(F32)).astype(BF16)
    r_hi = _dot(t_hi, wr_ref[...])
    r_lo = _dot(t_lo, wr_ref[...])
    logits = ((r_lo[:, LANES:] + r_lo[:, :LANES]) + r_hi[:, LANES:]) + r_hi[:, :LANES] + br_ref[...]
    lane = lax.broadcasted_iota(jnp.int32, logits.shape, 1)
    neg = -jnp.inf
    is_group = (lane >= N_EXPERTS) & (lane < N_EXPERTS + N_GROUPS)
    lg = jnp.where(is_group, logits, neg)
    lg_max = jnp.max(lg, axis=-1, keepdims=True)
    g_lane = jnp.min(jnp.where(lg == lg_max, lane, LANES), axis=-1, keepdims=True)
    p_top = 1.0 / jnp.sum(jnp.exp(lg - lg_max), axis=-1, keepdims=True)
    in_g = (lane // EXPERTS_PER_GROUP) == (g_lane - N_EXPERTS)
    le = jnp.where(in_g, logits, neg)
    v1 = jnp.max(le, axis=-1, keepdims=True)
    i1 = jnp.min(jnp.where(le == v1, lane, LANES), axis=-1, keepdims=True)
    le2 = jnp.where(lane == i1, neg, le)
    v2 = jnp.max(le2, axis=-1, keepdims=True)
    i2 = jnp.min(jnp.where(le2 == v2, lane, LANES), axis=-1, keepdims=True)
    e2 = jnp.exp(v2 - v1)
    w1 = p_top / (1.0 + e2)
    w2 = p_top * e2 / (1.0 + e2)
    gates_ref[...] = jnp.where(lane == i1, w1, 0.0) + jnp.where(lane == i2, w2, 0.0)


def _post(x, gla_o, diff_o, wo, g2, wr_parts, br, tm):
    t = x.shape[0]
    row = lambda w: pl.BlockSpec((tm, w), lambda i: (i, 0))
    full = lambda a: pl.BlockSpec(a.shape, lambda i: (0,) * a.ndim)
    return pl.pallas_call(
        _post_kernel,
        grid=(t // tm,),
        in_specs=[row(D_MODEL), row(GLA_V), row(DIFF_V), full(wo), full(g2), full(wr_parts), full(br)],
        out_specs=[row(D_MODEL), row(D_MODEL), row(LANES)],
        out_shape=[jax.ShapeDtypeStruct((t, D_MODEL), F32), jax.ShapeDtypeStruct((t, D_MODEL), BF16),
                   jax.ShapeDtypeStruct((t, LANES), F32)],
        compiler_params=_params(("parallel",)),
        name="post",
    )(x, gla_o, diff_o, wo, g2, wr_parts, br)


def _moe_kernel(t_ref, gates_ref, x1_ref, wgu_ref, wd_ref, gf_ref, y_ref, *, eps_per_step):
    e0 = pl.program_id(1)

    @pl.when(e0 == 0)
    def _():
        y_ref[...] = jnp.zeros_like(y_ref)

    t = t_ref[...]
    gates = gates_ref[...]
    ei = lax.broadcasted_iota(jnp.int32, gates.shape, 1)
    acc = y_ref[...]
    for k in range(eps_per_step):
        gu = _dot(t, wgu_ref[k])
        gate = jnp.sum(jnp.where(ei == e0 * eps_per_step + k, gates, 0.0), axis=-1, keepdims=True)
        hid = _silu(gu[:, :EXPERT_FF]) * gu[:, EXPERT_FF:] * gate
        acc = acc + _dot(hid.astype(BF16), wd_ref[k])
    y_ref[...] = acc

    @pl.when(e0 == pl.num_programs(1) - 1)
    def _():
        y_ref[...] = _rms(x1_ref[...] + y_ref[...], gf_ref[...])


def _moe(t, gates, x1, wgu, wd, gf, tm, eps_per_step=2 * EXPERTS_PER_GROUP):
    n = t.shape[0]
    row = lambda w: pl.BlockSpec((tm, w), lambda i, e: (i, 0))
    return pl.pallas_call(
        functools.partial(_moe_kernel, eps_per_step=eps_per_step),
        grid=(n // tm, N_EXPERTS // eps_per_step),
        in_specs=[row(D_MODEL), row(LANES), row(D_MODEL),
                  pl.BlockSpec((eps_per_step, D_MODEL, 2 * EXPERT_FF), lambda i, e: (e, 0, 0)),
                  pl.BlockSpec((eps_per_step, EXPERT_FF, D_MODEL), lambda i, e: (e, 0, 0)),
                  pl.BlockSpec((1, D_MODEL), lambda i, e: (0, 0))],
        out_specs=row(D_MODEL),
        out_shape=jax.ShapeDtypeStruct((n, D_MODEL), F32),
        compiler_params=_params(("parallel", "arbitrary")),
        name="moe",
    )(t, gates, x1, wgu, wd, gf)


def kernel(x_prompt, x_sample, cache_k, cache_v, state_gla, page_table, norm1_g, w_in, w_a2, b_a, gla_norm_g,
           lambda_q1, lambda_k1, lambda_q2, lambda_k2, diff_norm_g, w_out, norm2_g, w_router_g, b_router_g,
           w_router_e, b_router_e, w_gate, w_up, w_down, norm_f_g):
    nb, seq, _ = x_prompt.shape
    ns = x_sample.shape[0]
    depth = w_in.shape[0]
    assert depth == 1 and x_sample.shape[1] == 1
    lam_init = 0.8 - 0.6 * math.exp(-0.3 * 0)

    c_code = 2 * GLA_QK + 2 * GLA_V
    w = w_in[0]
    wm = jnp.concatenate([w[:, :c_code], w[:, c_code + GLA_RANK:]], axis=1).astype(BF16)
    wc = jnp.pad(w[:, c_code:c_code + GLA_RANK], ((0, 0), (0, LANES - GLA_RANK))).astype(BF16)
    wa2p = jnp.pad(w_a2[0], ((0, LANES - GLA_RANK), (0, 0))).astype(BF16)
    ba = b_a[0].reshape(1, GLA_QK)
    g1 = norm1_g[0].reshape(1, D_MODEL)
    g2 = norm2_g[0].reshape(1, D_MODEL)
    gf = norm_f_g.reshape(1, D_MODEL)
    gla_g = gla_norm_g[0].reshape(1, GLA_DV)
    diff_g = diff_norm_g[0].reshape(1, DIFF_V_DIM)
    lams = [a[0].reshape(1, DIFF_HEAD_DIM) for a in (lambda_q1, lambda_k1, lambda_q2, lambda_k2)]
    wo = w_out[0].astype(BF16)
    wr = jnp.concatenate([w_router_e[0].transpose(1, 0, 2).reshape(D_MODEL, N_EXPERTS), w_router_g[0]], axis=1)
    wr = jnp.pad(wr, ((0, 0), (0, LANES - N_GROUPS - N_EXPERTS)))
    wr_hi = wr.astype(BF16)
    wr_lo = (wr - wr_hi.astype(F32)).astype(BF16)
    wr_parts = jnp.concatenate([wr_hi, wr_lo], axis=1)
    br = jnp.pad(jnp.concatenate([b_router_e[0].reshape(N_EXPERTS), b_router_g[0]]),
                 (0, LANES - N_GROUPS - N_EXPERTS)).reshape(1, LANES)
    wgu = jnp.concatenate([w_gate[0], w_up[0]], axis=-1).astype(BF16)
    wd = w_down[0].astype(BF16)

    def dense_tail(x, gla_o, diff_o, tm_post, tm_moe):
        x1, t, gates = _post(x, gla_o, diff_o, wo, g2, wr_parts, br, tm_post)
        return _moe(t, gates, x1, wgu, wd, gf, tm_moe)

    xp = x_prompt.reshape(nb * seq, D_MODEL)
    gq, gk, la, gv, gr, dqb, dk, dv, dkb, vt = _inproj(xp, g1, wm, wc, wa2p, ba, 512, (nb, seq))
    s0 = jnp.zeros((nb, GLA_HEADS, GLA_DK, GLA_DV), F32)
    gla_o, s_prompt = _gla_prompt(gq, gk, la, gv, gr, gla_g, s0, nb, 256)
    diff_o = _attn_prompt(lams, dqb, dkb, vt, diff_g, nb, seq, lam_init)
    y_prompt = dense_tail(xp, gla_o, diff_o, 1024, 1024).reshape(nb, seq, D_MODEL)

    xs = x_sample.reshape(ns, D_MODEL)
    sgq, sgk, sla, sgv, sgr, sdqb, sdk, sdv = _inproj(xs, g1, wm, wc, wa2p, ba, ns)
    sgla_o, s_sample = _gla_step(sgq, sgk, sla, sgv, sgr, gla_g, state_gla[0])
    sdiff_o = _attn_decode(page_table, lams, sdqb, sdk, sdv, diff_g, cache_k, cache_v, lam_init, 8)
    y_sample = dense_tail(xs, sgla_o.reshape(ns, GLA_V), sdiff_o.reshape(ns, DIFF_V), ns, ns).reshape(ns, 1, D_MODEL)

    k_prompt = dk.reshape(1, nb, seq, DIFF_HEADS, 2 * DIFF_HEAD_DIM)
    v_prompt = dv.reshape(1, nb, seq, DIFF_HEADS, DIFF_V_DIM)
    k_sample = sdk.reshape(1, ns, 1, DIFF_HEADS, 2 * DIFF_HEAD_DIM)
    v_sample = sdv.reshape(1, ns, 1, DIFF_HEADS, DIFF_V_DIM)
    return (y_prompt, y_sample, k_prompt, v_prompt, s_prompt[None], k_sample, v_sample, s_sample[None])
```

```python
import functools
import math

import jax
import jax.numpy as jnp
from jax import lax
from jax.experimental import pallas as pl
from jax.experimental.pallas import tpu as pltpu

F32 = jnp.float32
BF16 = jnp.bfloat16

D_MODEL = 1024
GLA_HEADS = 4
GLA_DK = 64
GLA_DV = 128
GLA_RANK = 16
GLA_TAU = 16.0
GLA_CHUNK = 16
DIFF_HEADS = 4
DIFF_HEAD_DIM = 64
DIFF_V_DIM = 128
GLA_QK = GLA_HEADS * GLA_DK
GLA_V = GLA_HEADS * GLA_DV
DIFF_QK = DIFF_HEADS * 2 * DIFF_HEAD_DIM
DIFF_V = DIFF_HEADS * DIFF_V_DIM
N_GROUPS = 4
EXPERTS_PER_GROUP = 4
N_EXPERTS = 16
EXPERT_FF = 256
EPS = 1e-6
LANES = 128
BF16_SUBLANES = 16
LOG2E = math.log2(math.e)
DECODE_SLOTS = 3
VT_ROWS = DIFF_V_DIM + BF16_SUBLANES
VMEM_LIMIT = 56 * 1024 * 1024

_C_GQ, _C_GK, _C_GV, _C_GR, _C_CODE, _C_DQ, _C_DK, _C_DV, _C_END = 0, 256, 512, 1024, 1536, 1552, 2064, 2576, 3088


def _rms(x, g):
    return x * lax.rsqrt(jnp.mean(x * x, axis=-1, keepdims=True) + EPS) * g


def _silu(x):
    return x * (1.0 / (1.0 + jnp.exp(-x)))


def _log_sigmoid(x):
    return jnp.minimum(x, 0.0) - jnp.log1p(jnp.exp(-jnp.abs(x)))


def _dot(a, b):
    return jnp.dot(a, b, preferred_element_type=F32)


def _dot_nt(a, b):
    return lax.dot_general(a, b, (((1,), (1,)), ((), ())), preferred_element_type=F32)


def _split3(x):
    hi = x.astype(BF16)
    r = x - hi.astype(F32)
    mid = r.astype(BF16)
    lo = (r - mid.astype(F32)).astype(BF16)
    return hi, mid, lo


def _params(sem):
    return pltpu.CompilerParams(dimension_semantics=sem, vmem_limit_bytes=VMEM_LIMIT)


def _inproj_kernel(x_ref, g_ref, wt_ref, wa2_ref, ba_ref,
                   gq_ref, gk_ref, la_ref, gv_ref, gr_ref, dqb_ref, dk_ref, dv_ref, *prompt_refs):
    h = _rms(x_ref[...], g_ref[...]).astype(BF16)

    def seg(lo, hi):
        return _dot_nt(h, wt_ref[lo:hi, :])

    gq_ref[...] = seg(_C_GQ, _C_GK) * (GLA_DK ** -0.5)
    gk_ref[...] = seg(_C_GK, _C_GV)
    gv_ref[...] = seg(_C_GV, _C_GR)
    gr_ref[...] = seg(_C_GR, _C_CODE)
    dqb_ref[...] = (seg(_C_DQ, _C_DK) * (DIFF_HEAD_DIM ** -0.5 * LOG2E)).astype(BF16)
    dk = seg(_C_DK, _C_DV)
    dv = seg(_C_DV, _C_END)
    tm = dk.shape[0]
    for hd in range(DIFF_HEADS):
        rows = pl.ds(hd, tm, stride=DIFF_HEADS)
        dk_ref[rows, :] = dk[:, hd * LANES:(hd + 1) * LANES]
        dv_ref[rows, :] = dv[:, hd * LANES:(hd + 1) * LANES]
    if prompt_refs:
        dkb_ref, vt_ref = prompt_refs
        dkb_ref[...] = dk.astype(BF16)
        ones_row = (lax.broadcasted_iota(jnp.int32, (VT_ROWS - DIFF_V_DIM, tm), 0) == 0).astype(F32)
        for hd in range(DIFF_HEADS):
            vt_ref[0, hd, :DIFF_V_DIM, :] = dv[:, hd * DIFF_V_DIM:(hd + 1) * DIFF_V_DIM].T.astype(BF16)
            vt_ref[0, hd, DIFF_V_DIM:, :] = ones_row.astype(BF16)
    code = seg(_C_CODE, _C_CODE + LANES)
    pre = _dot(code.astype(BF16), wa2_ref[...]) + ba_ref[...]
    la_ref[...] = _log_sigmoid(pre) * (1.0 / GLA_TAU)


def _inproj(x, g1, wt, wa2p, ba, tm, prompt_shape=None):
    t = x.shape[0]
    row = lambda w: pl.BlockSpec((tm, w), lambda i: (i, 0))
    full = lambda a: pl.BlockSpec(a.shape, lambda i: (0,) * a.ndim)
    outs = [(GLA_QK, F32), (GLA_QK, F32), (GLA_QK, F32), (GLA_V, F32), (GLA_V, F32), (DIFF_QK, BF16)]
    out_specs = [row(w) for w, _ in outs]
    out_shape = [jax.ShapeDtypeStruct((t, w), dt) for w, dt in outs]
    out_specs += [pl.BlockSpec((tm * DIFF_HEADS, LANES), lambda i: (i, 0))] * 2
    out_shape += [jax.ShapeDtypeStruct((t * DIFF_HEADS, LANES), F32)] * 2
    if prompt_shape is not None:
        nb, seq = prompt_shape
        nl = seq // tm
        out_specs += [row(DIFF_QK), pl.BlockSpec((1, DIFF_HEADS, VT_ROWS, tm), lambda i: (i // nl, 0, 0, i % nl))]
        out_shape += [jax.ShapeDtypeStruct((t, DIFF_QK), BF16),
                      jax.ShapeDtypeStruct((nb, DIFF_HEADS, VT_ROWS, seq), BF16)]
    return pl.pallas_call(
        _inproj_kernel,
        grid=(t // tm,),
        in_specs=[row(D_MODEL), full(g1), full(wt), full(wa2p), full(ba)],
        out_specs=out_specs,
        out_shape=out_shape,
        compiler_params=_params(("parallel",)),
        name="inproj",
    )(x, g1, wt, wa2p, ba)


def _dot_tn(a, b):
    return lax.dot_general(a, b, (((0,), (0,)), ((), ())), preferred_element_type=F32)


def _gla_kernel(gq_ref, gk_ref, la_ref, gv_ref, gr_ref, tri_ref, last_ref, csel_ref, ind_ref, rep_ref, same_ref, g_ref, s0_ref,
                o_ref, sfin_ref,
                s_s, b_s, qi_s, kd_s, adec_s, o_s, p_s):
    l = pl.program_id(1)
    tl = gq_ref.shape[0]
    c16 = GLA_CHUNK

    @pl.when(l == 0)
    def _():
        for h in range(GLA_HEADS):
            s_s[h * GLA_DK:(h + 1) * GLA_DK, :] = s0_ref[0, h]

    hi, mid, lo = _split3(la_ref[...])

    def cs(m):
        return (_dot(m, lo) + _dot(m, mid)) + _dot(m, hi)

    b = cs(tri_ref[...])
    bl = cs(last_ref[...])
    b_s[...] = b
    qi_s[...] = gq_ref[...] * jnp.exp(b)
    kd_s[...] = (gk_ref[...] * jnp.exp(bl - b)).astype(BF16)
    csel = csel_ref[...]
    adec_s[...] = jnp.exp((_dot_tn(lo, csel) + _dot_tn(mid, csel)) + _dot_tn(hi, csel))

    rowi = lax.broadcasted_iota(jnp.int32, (c16, GLA_QK), 0)
    vb = gv_ref[...].astype(BF16)

    for c in range(tl // c16):
        r0 = c * c16
        q_c = gq_ref[r0:r0 + c16, :]
        b_c = b[r0:r0 + c16, :]
        for j in range(c16):
            bj = b_s[r0 + j:r0 + j + 1, :]
            kj = gk_ref[r0 + j:r0 + j + 1, :]
            e = jnp.exp(jnp.where(rowi >= j, b_c - bj, -jnp.inf))
            p_s[r0:r0 + c16, j * GLA_QK:(j + 1) * GLA_QK] = (q_c * e * kj).astype(BF16)
    scores = _dot(p_s[...], ind_ref[...]).astype(BF16)
    a_bd = _dot(scores, rep_ref[...]).astype(BF16) * same_ref[...]
    for h in range(GLA_HEADS):
        dv = slice(h * GLA_DV, (h + 1) * GLA_DV)
        o_s[:, dv] = _dot(a_bd[:, h * tl:(h + 1) * tl], vb[:, dv])

    adec = adec_s[...]
    state = [s_s[h * GLA_DK:(h + 1) * GLA_DK, :] for h in range(GLA_HEADS)]
    for c in range(tl // c16):
        rows = slice(c * c16, (c + 1) * c16)
        qi_c = qi_s[rows, :].astype(BF16)
        kd_c = kd_s[rows, :]
        vb_c = vb[rows, :]
        a_col = adec[:, c:c + 1]
        o_parts = []
        for h in range(GLA_HEADS):
            dk = slice(h * GLA_DK, (h + 1) * GLA_DK)
            dv = slice(h * GLA_DV, (h + 1) * GLA_DV)
            o_parts.append(_dot(qi_c[:, dk], state[h].astype(BF16)))
            state[h] = a_col[dk] * state[h] + _dot_tn(kd_c[:, dk], vb_c[:, dv])
        o_s[rows, :] += jnp.concatenate(o_parts, axis=1)
    for h in range(GLA_HEADS):
        s_s[h * GLA_DK:(h + 1) * GLA_DK, :] = state[h]

    o = o_s[...]
    for h in range(GLA_HEADS):
        sl = slice(h * GLA_DV, (h + 1) * GLA_DV)
        o_ref[:, sl] = (_rms(o[:, sl], g_ref[...]) * _silu(gr_ref[:, sl])).astype(o_ref.dtype)

    @pl.when(l == pl.num_programs(1) - 1)
    def _():
        for h in range(GLA_HEADS):
            sfin_ref[0, h] = s_s[h * GLA_DK:(h + 1) * GLA_DK, :]


def _gla_prompt(gq, gk, la, gv, gr, gla_g, s0, nb, tl):
    t = gq.shape[0]
    nl = t // nb // tl
    ti = jnp.arange(tl)
    same = (ti[:, None] // GLA_CHUNK) == (ti[None, :] // GLA_CHUNK)
    tri = (same & (ti[None, :] <= ti[:, None])).astype(BF16)
    last = same.astype(BF16)
    csel = ((ti[:, None] // GLA_CHUNK) == jnp.arange(LANES)[None, :]).astype(BF16)
    kj = jnp.arange(GLA_CHUNK * GLA_QK)
    ln = jnp.arange(LANES)
    ind = ((kj[:, None] // GLA_QK == ln[None, :] % GLA_CHUNK)
           & ((kj[:, None] % GLA_QK) // GLA_DK == ln[None, :] // GLA_CHUNK)).astype(BF16)
    ks = jnp.arange(GLA_HEADS * tl)
    rep = ((ln[:, None] // GLA_CHUNK == ks[None, :] // tl)
           & (ln[:, None] % GLA_CHUNK == ks[None, :] % GLA_CHUNK)).astype(BF16)
    same_chunk = (ti[:, None] // GLA_CHUNK == (ks[None, :] % tl) // GLA_CHUNK).astype(BF16)
    row = lambda w: pl.BlockSpec((tl, w), lambda b, l: (b * nl + l, 0))
    full = lambda shp: pl.BlockSpec(shp, lambda b, l: (0,) * len(shp))
    st = pl.BlockSpec((1, GLA_HEADS, GLA_DK, GLA_DV), lambda b, l: (b, 0, 0, 0))
    return pl.pallas_call(
        _gla_kernel,
        grid=(nb, nl),
        in_specs=[row(GLA_QK), row(GLA_QK), row(GLA_QK), row(GLA_V), row(GLA_V),
                  full((tl, tl)), full((tl, tl)), full((tl, LANES)), full(ind.shape), full(rep.shape), full(same_chunk.shape),
                  full((1, GLA_DV)), st],
        out_specs=[row(GLA_V), st],
        out_shape=[jax.ShapeDtypeStruct((t, GLA_V), BF16),
                   jax.ShapeDtypeStruct((nb, GLA_HEADS, GLA_DK, GLA_DV), F32)],
        scratch_shapes=[pltpu.VMEM((GLA_QK, GLA_DV), F32), pltpu.VMEM((tl, GLA_QK), F32),
                        pltpu.VMEM((tl, GLA_QK), F32), pltpu.VMEM((tl, GLA_QK), BF16),
                        pltpu.VMEM((GLA_QK, LANES), F32),
                        pltpu.VMEM((tl, GLA_V), F32), pltpu.VMEM((tl, GLA_CHUNK * GLA_QK), BF16)],
        compiler_params=_params(("parallel", "arbitrary")),
        name="gla_prompt",
    )(gq, gk, la, gv, gr, tri, last, csel, ind, rep, same_chunk, gla_g, s0)


def _gla_step_kernel(q_ref, k_ref, la_ref, v_ref, gr_ref, g_ref, s_ref, o_ref, snew_ref):
    eye = (lax.broadcasted_iota(jnp.int32, (GLA_QK, GLA_QK), 0)
           == lax.broadcasted_iota(jnp.int32, (GLA_QK, GLA_QK), 1))

    def column(row):
        return jnp.sum(jnp.where(eye, row, 0.0), axis=1, keepdims=True)

    for i in range(q_ref.shape[0]):
        q_col, k_col, a_col = column(q_ref[i]), column(k_ref[i]), jnp.exp(column(la_ref[i]))
        for h in range(GLA_HEADS):
            dk = slice(h * GLA_DK, (h + 1) * GLA_DK)
            dv = slice(h * GLA_DV, (h + 1) * GLA_DV)
            s_new = a_col[dk] * s_ref[i, h] + k_col[dk] * v_ref[i, :, dv]
            snew_ref[i, h] = s_new
            o = jnp.sum(q_col[dk] * s_new, axis=0, keepdims=True)
            o_ref[i, :, dv] = _rms(o, g_ref[...]) * _silu(gr_ref[i, :, dv])


def _gla_step(gq, gk, la, gv, gr, gla_g, state, per_step=8):
    n = gq.shape[0]
    assert n % per_step == 0
    col = pl.BlockSpec((per_step, 1, GLA_QK), lambda b: (b, 0, 0))
    rowv = pl.BlockSpec((per_step, 1, GLA_V), lambda b: (b, 0, 0))
    st = pl.BlockSpec((per_step, GLA_HEADS, GLA_DK, GLA_DV), lambda b: (b, 0, 0, 0))
    return pl.pallas_call(
        _gla_step_kernel,
        grid=(n // per_step,),
        in_specs=[col, col, col, rowv, rowv, pl.BlockSpec((1, GLA_DV), lambda b: (0, 0)), st],
        out_specs=[rowv, st],
        out_shape=[jax.ShapeDtypeStruct((n, 1, GLA_V), F32),
                   jax.ShapeDtypeStruct((n, GLA_HEADS, GLA_DK, GLA_DV), F32)],
        compiler_params=_params(("parallel",)),
        name="gla_step",
    )(gq.reshape(n, 1, GLA_QK), gk.reshape(n, 1, GLA_QK), la.reshape(n, 1, GLA_QK),
      gv.reshape(n, 1, GLA_V), gr.reshape(n, 1, GLA_V), gla_g, state)


def _lambda(lq1_ref, lk1_ref, lq2_ref, lk2_ref, lam_init):
    e1 = jnp.exp(jnp.sum(lq1_ref[...] * lk1_ref[...], axis=-1, keepdims=True))
    e2 = jnp.exp(jnp.sum(lq2_ref[...] * lk2_ref[...], axis=-1, keepdims=True))
    return e1 - e2 + lam_init


def _softmax_step(s, v, m, l, acc):
    m_new = jnp.maximum(m, jnp.max(s, axis=-1, keepdims=True))
    alpha = jnp.exp2(m - m_new)
    p = jnp.exp2(s - m_new)
    l = alpha * l + jnp.sum(p, axis=-1, keepdims=True)
    acc = alpha * acc + _dot(p.astype(BF16), v)
    return m_new, l, acc


def _attn_kernel(lq1_ref, lk1_ref, lq2_ref, lk2_ref, q_ref, k_ref, vt_ref, g_ref, o_ref, s_s, a_s,
                 *, lam_init, tq, ks, unroll):
    seq = q_ref.shape[0]
    lane = lax.broadcasted_iota(jnp.int32, (tq, LANES), 1)
    rel = lax.broadcasted_iota(jnp.int32, (ks, tq), 0) - lax.broadcasted_iota(jnp.int32, (ks, tq), 1)
    lam = _lambda(lq1_ref, lk1_ref, lq2_ref, lk2_ref, lam_init)
    n_mask = max(1, tq // ks)

    def q_tile(i, _):
        q0 = pl.multiple_of(i * tq, tq)
        q = q_ref[pl.ds(q0, tq), :]
        zero = jnp.zeros_like(q)
        qs = (jnp.where(lane < DIFF_HEAD_DIM, q, zero), jnp.where(lane >= DIFF_HEAD_DIM, q, zero))
        a_s[...] = jnp.zeros_like(a_s)
        n_full = (q0 + 1) // ks

        def score(j, slot, lo):
            k = k_ref[pl.ds(pl.multiple_of(j * ks, ks), ks), :]
            for mp in range(2):
                s_s[slot, mp, :, lo:] = _dot_nt(k, qs[mp][lo:])

        def step(j, cur, ms, masked, lo=0, lo_next=0):
            if lo_next is not None:
                score(j + 1, 1 - cur, lo_next)
            k0 = pl.multiple_of(j * ks, ks)
            vt = vt_ref[0, 0, :, pl.ds(k0, ks)]
            out = []
            for mp in range(2):
                s = s_s[cur, mp, :, lo:]
                if masked:
                    s = jnp.where(rel[:, lo:] <= q0 - k0, s, -jnp.inf)
                m_old = ms[mp][:, lo:]
                m_new = jnp.maximum(m_old, jnp.max(s, axis=0, keepdims=True))
                p = jnp.exp2(s - m_new).astype(BF16)
                a_s[mp, :, lo:] = jnp.exp2(m_old - m_new) * a_s[mp, :, lo:] + _dot(vt, p)
                out.append(m_new if lo == 0 else jnp.concatenate([ms[mp][:, :lo], m_new], axis=1))
            return tuple(out)

        minf = jnp.full((1, tq), -jnp.inf, F32)
        score(0, 0, 0)

        def trip(jj, ms):
            for u in range(unroll):
                ms = step(unroll * jj + u, u % 2, ms, False)
            return ms

        ms = lax.fori_loop(0, n_full // unroll, trip, (minf, minf))
        for r in range(n_mask):
            lo_next = (r + 1) * ks if r + 1 < n_mask else None
            ms = step(n_full + r, r % 2, ms, True, r * ks, lo_next)
        a1, a2 = a_s[0], a_s[1]
        w = a1[:DIFF_V_DIM] / a1[DIFF_V_DIM:DIFF_V_DIM + 1] - lam * (a2[:DIFF_V_DIM] / a2[DIFF_V_DIM:DIFF_V_DIM + 1])
        o_ref[pl.ds(q0, tq), :] = (_rms(w.T, g_ref[...]) * (1.0 - lam_init)).astype(o_ref.dtype)
        return 0

    lax.fori_loop(0, seq // tq, q_tile, 0)


def _attn_prompt(lams, dqb, dkb, vt, diff_g, nb, seq, lam_init, tq=1024, ks=256, unroll=4):
    t = dqb.shape[0]
    assert unroll % 2 == 0 and tq % (unroll * ks) == 0 and seq % tq == 0
    lam_spec = pl.BlockSpec((1, DIFF_HEAD_DIM), lambda b, h: (0, 0))
    seqspec = pl.BlockSpec((seq, LANES), lambda b, h: (b, h))
    return pl.pallas_call(
        functools.partial(_attn_kernel, lam_init=lam_init, tq=tq, ks=ks, unroll=unroll),
        grid=(nb, DIFF_HEADS),
        in_specs=[lam_spec] * 4 + [seqspec, seqspec, pl.BlockSpec((1, 1, VT_ROWS, seq), lambda b, h: (b, h, 0, 0)),
                                  pl.BlockSpec((1, DIFF_V_DIM), lambda b, h: (0, 0))],
        out_specs=seqspec,
        out_shape=jax.ShapeDtypeStruct((t, DIFF_V), BF16),
        scratch_shapes=[pltpu.VMEM((2, 2, ks, tq), F32), pltpu.VMEM((2, VT_ROWS, tq), F32)],
        compiler_params=_params(("parallel", "parallel")),
        name="attn_prompt",
    )(*lams, dqb, dkb, vt, diff_g)


def _decode_kernel(pt_ref, lq1_ref, lk1_ref, lq2_ref, lk2_ref, q_ref, kn_ref, vn_ref, g_ref, ck_ref, cv_ref,
                   o_ref, kbuf, vbuf, sem, *, lam_init, ppc, nch, nseq, page_rows):
    b = pl.program_id(0)
    nrow = 2 * DIFF_HEADS

    def copies(n):
        bb, cc, sl = n // nch, lax.rem(n, nch), lax.rem(n, DECODE_SLOTS)
        out = []
        for p in range(ppc):
            page = pt_ref[bb, cc * ppc + p]
            dst = pl.ds(p * page_rows, page_rows)
            out.append(pltpu.make_async_copy(ck_ref.at[page], kbuf.at[sl, dst, :], sem.at[0, sl]))
            out.append(pltpu.make_async_copy(cv_ref.at[page], vbuf.at[sl, dst, :], sem.at[1, sl]))
        return out

    ahead = DECODE_SLOTS - 1

    @pl.when(b == 0)
    def _():
        for n in range(ahead):
            for cp in copies(jnp.int32(n)):
                cp.start()

    rowi = lax.broadcasted_iota(jnp.int32, (nrow, LANES), 0)
    lane = lax.broadcasted_iota(jnp.int32, (nrow, LANES), 1)
    q4 = q_ref[0].astype(F32)
    q8 = jnp.concatenate([q4, q4], axis=0)
    qm = jnp.where((rowi < DIFF_HEADS) == (lane < DIFF_HEAD_DIM), q8, 0.0)
    qmb = qm.astype(BF16)
    keys = ppc * page_rows
    col = lax.broadcasted_iota(jnp.int32, (nrow, keys), 1)
    rowk = lax.broadcasted_iota(jnp.int32, (nrow, keys), 0)
    own_head = (col % DIFF_HEADS) == (rowk % DIFF_HEADS)

    def chunk(c, carry):
        step = b * nch + c
        slot = lax.rem(step, DECODE_SLOTS)

        @pl.when(step + ahead < nseq * nch)
        def _():
            for i, cp in enumerate(copies(step + ahead)):
                cp.start(priority=i % 2)

        for cp in copies(step):
            cp.wait()
        s = _dot_nt(qmb, kbuf[slot].astype(BF16))
        s = jnp.where(own_head, s, -jnp.inf)
        return _softmax_step(s, vbuf[slot].astype(BF16), *carry)

    init = (jnp.full((nrow, 1), -jnp.inf, F32), jnp.zeros((nrow, 1), F32), jnp.zeros((nrow, DIFF_V_DIM), F32))
    m, l, acc = lax.fori_loop(0, nch, chunk, init)

    lam = _lambda(lq1_ref, lk1_ref, lq2_ref, lk2_ref, lam_init)
    kn = kn_ref[0].astype(BF16).astype(F32)
    vn = vn_ref[0].astype(BF16).astype(F32)
    sn = jnp.sum(qm * jnp.concatenate([kn, kn], axis=0), axis=-1, keepdims=True)
    m_new = jnp.maximum(m, sn)
    alpha = jnp.exp2(m - m_new)
    p = jnp.exp2(sn - m_new)
    lf = alpha * l + p
    w = (alpha * acc + p.astype(BF16).astype(F32) * jnp.concatenate([vn, vn], axis=0)) / lf
    out = w[:DIFF_HEADS] - lam * w[DIFF_HEADS:]
    o_ref[0] = _rms(out, g_ref[...]) * (1.0 - lam_init)


def _attn_decode(page_table, lams, dqb, dk, dv, diff_g, cache_k, cache_v, lam_init, ppc):
    n, n_pages = page_table.shape
    nch = n_pages // ppc
    n_pool, page = cache_k.shape[1], cache_k.shape[2]
    page_rows = page * DIFF_HEADS
    ck = cache_k.reshape(n_pool, page_rows, 2 * DIFF_HEAD_DIM)
    cv = cache_v.reshape(n_pool, page_rows, DIFF_V_DIM)
    lam_spec = pl.BlockSpec((1, DIFF_HEAD_DIM), lambda b, pt: (0, 0))
    rowspec = pl.BlockSpec((1, DIFF_HEADS, LANES), lambda b, pt: (b, 0, 0))
    anyspec = pl.BlockSpec(memory_space=pl.ANY)
    kern = functools.partial(_decode_kernel, lam_init=lam_init, ppc=ppc, nch=nch, nseq=n, page_rows=page_rows)
    return pl.pallas_call(
        kern,
        grid_spec=pltpu.PrefetchScalarGridSpec(
            num_scalar_prefetch=1,
            grid=(n,),
            in_specs=[lam_spec] * 4 + [rowspec, rowspec, rowspec,
                                      pl.BlockSpec((1, DIFF_V_DIM), lambda b, pt: (0, 0)), anyspec, anyspec],
            out_specs=rowspec,
            scratch_shapes=[pltpu.VMEM((DECODE_SLOTS, ppc * page_rows, LANES), F32),
                            pltpu.VMEM((DECODE_SLOTS, ppc * page_rows, LANES), F32),
                            pltpu.SemaphoreType.DMA((2, DECODE_SLOTS))]),
        out_shape=jax.ShapeDtypeStruct((n, DIFF_HEADS, LANES), F32),
        compiler_params=_params(("arbitrary",)),
        name="attn_decode",
    )(page_table, *lams, dqb.reshape(n, DIFF_HEADS, LANES), dk.reshape(n, DIFF_HEADS, LANES),
      dv.reshape(n, DIFF_HEADS, LANES), diff_g, ck, cv)


def _post_kernel(x_ref, go_ref, do_ref, wo_ref, g2_ref, wr_ref, br_ref, x1_ref, t_ref, gates_ref):
    x1 = x_ref[...] + _dot(go_ref[...].astype(BF16), wo_ref[:GLA_V, :]) + _dot(do_ref[...].astype(BF16), wo_ref[GLA_V:, :])
    x1_ref[...] = x1
    t = _rms(x1, g2_ref[...])
    t_ref[...] = t.astype(BF16)
    t_hi = t.astype(BF16)
    t_lo = (t - t_hi.astype(F32)).astype(BF16)
    r_hi = _dot(t_hi, wr_ref[...])
    r_lo = _dot(t_lo, wr_ref[...])
    logits = ((r_lo[:, LANES:] + r_lo[:, :LANES]) + r_hi[:, LANES:]) + r_hi[:, :LANES] + br_ref[...]
    lane = lax.broadcasted_iota(jnp.int32, logits.shape, 1)
    neg = -jnp.inf
    is_group = (lane >= N_EXPERTS) & (lane < N_EXPERTS + N_GROUPS)
    lg = jnp.where(is_group, logits, neg)
    lg_max = jnp.max(lg, axis=-1, keepdims=True)
    g_lane = jnp.min(jnp.where(lg == lg_max, lane, LANES), axis=-1, keepdims=True)
    p_top = 1.0 / jnp.sum(jnp.exp(lg - lg_max), axis=-1, keepdims=True)
    in_g = (lane // EXPERTS_PER_GROUP) == (g_lane - N_EXPERTS)
    le = jnp.where(in_g, logits, neg)
    v1 = jnp.max(le, axis=-1, keepdims=True)
    i1 = jnp.min(jnp.where(le == v1, lane, LANES), axis=-1, keepdims=True)
    le2 = jnp.where(lane == i1, neg, le)
    v2 = jnp.max(le2, axis=-1, keepdims=True)
    i2 = jnp.min(jnp.where(le2 == v2, lane, LANES), axis=-1, keepdims=True)
    e2 = jnp.exp(v2 - v1)
    w1 = p_top / (1.0 + e2)
    w2 = p_top * e2 / (1.0 + e2)
    gates_ref[...] = jnp.where(lane == i1, w1, 0.0) + jnp.where(lane == i2, w2, 0.0)


def _post(x, gla_o, diff_o, wo, g2, wr_parts, br, tm):
    t = x.shape[0]
    row = lambda w: pl.BlockSpec((tm, w), lambda i: (i, 0))
    full = lambda a: pl.BlockSpec(a.shape, lambda i: (0,) * a.ndim)
    return pl.pallas_call(
        _post_kernel,
        grid=(t // tm,),
        in_specs=[row(D_MODEL), row(GLA_V), row(DIFF_V), full(wo), full(g2), full(wr_parts), full(br)],
        out_specs=[row(D_MODEL), row(D_MODEL), row(LANES)],
        out_shape=[jax.ShapeDtypeStruct((t, D_MODEL), F32), jax.ShapeDtypeStruct((t, D_MODEL), BF16),
                   jax.ShapeDtypeStruct((t, LANES), F32)],
        compiler_params=_params(("parallel",)),
        name="post",
    )(x, gla_o, diff_o, wo, g2, wr_parts, br)


def _moe_kernel(t_ref, gates_ref, x1_ref, wgu_ref, wd_ref, gf_ref, y_ref, *, eps_per_step):
    e0 = pl.program_id(1)

    @pl.when(e0 == 0)
    def _():
        y_ref[...] = jnp.zeros_like(y_ref)

    t = t_ref[...]
    gates = gates_ref[...]
    ei = lax.broadcasted_iota(jnp.int32, gates.shape, 1)
    acc = y_ref[...]
    for k in range(eps_per_step):
        gu = _dot(t, wgu_ref[k])
        gate = jnp.sum(jnp.where(ei == e0 * eps_per_step + k, gates, 0.0), axis=-1, keepdims=True)
        hid = _silu(gu[:, :EXPERT_FF]) * gu[:, EXPERT_FF:] * gate
        acc = acc + _dot(hid.astype(BF16), wd_ref[k])
    y_ref[...] = acc

    @pl.when(e0 == pl.num_programs(1) - 1)
    def _():
        y_ref[...] = _rms(x1_ref[...] + y_ref[...], gf_ref[...])


def _moe(t, gates, x1, wgu, wd, gf, tm, eps_per_step=2 * EXPERTS_PER_GROUP):
    n = t.shape[0]
    row = lambda w: pl.BlockSpec((tm, w), lambda i, e: (i, 0))
    return pl.pallas_call(
        functools.partial(_moe_kernel, eps_per_step=eps_per_step),
        grid=(n // tm, N_EXPERTS // eps_per_step),
        in_specs=[row(D_MODEL), row(LANES), row(D_MODEL),
                  pl.BlockSpec((eps_per_step, D_MODEL, 2 * EXPERT_FF), lambda i, e: (e, 0, 0)),
                  pl.BlockSpec((eps_per_step, EXPERT_FF, D_MODEL), lambda i, e: (e, 0, 0)),
                  pl.BlockSpec((1, D_MODEL), lambda i, e: (0, 0))],
        out_specs=row(D_MODEL),
        out_shape=jax.ShapeDtypeStruct((n, D_MODEL), F32),
        compiler_params=_params(("parallel", "arbitrary")),
        name="moe",
    )(t, gates, x1, wgu, wd, gf)


def kernel(x_prompt, x_sample, cache_k, cache_v, state_gla, page_table, norm1_g, w_in, w_a2, b_a, gla_norm_g,
           lambda_q1, lambda_k1, lambda_q2, lambda_k2, diff_norm_g, w_out, norm2_g, w_router_g, b_router_g,
           w_router_e, b_router_e, w_gate, w_up, w_down, norm_f_g):
    nb, seq, _ = x_prompt.shape
    ns = x_sample.shape[0]
    depth = w_in.shape[0]
    assert depth == 1 and x_sample.shape[1] == 1
    lam_init = 0.8 - 0.6 * math.exp(-0.3 * 0)

    wt = w_in[0].T.astype(BF16)
    wa2p = jnp.pad(w_a2[0], ((0, LANES - GLA_RANK), (0, 0))).astype(BF16)
    ba = b_a[0].reshape(1, GLA_QK)
    g1 = norm1_g[0].reshape(1, D_MODEL)
    g2 = norm2_g[0].reshape(1, D_MODEL)
    gf = norm_f_g.reshape(1, D_MODEL)
    gla_g = gla_norm_g[0].reshape(1, GLA_DV)
    diff_g = diff_norm_g[0].reshape(1, DIFF_V_DIM)
    lams = [a[0].reshape(1, DIFF_HEAD_DIM) for a in (lambda_q1, lambda_k1, lambda_q2, lambda_k2)]
    wo = w_out[0].astype(BF16)
    wr = jnp.concatenate([w_router_e[0].transpose(1, 0, 2).reshape(D_MODEL, N_EXPERTS), w_router_g[0]], axis=1)
    wr = jnp.pad(wr, ((0, 0), (0, LANES - N_GROUPS - N_EXPERTS)))
    wr_hi = wr.astype(BF16)
    wr_lo = (wr - wr_hi.astype(F32)).astype(BF16)
    wr_parts = jnp.concatenate([wr_hi, wr_lo], axis=1)
    br = jnp.pad(jnp.concatenate([b_router_e[0].reshape(N_EXPERTS), b_router_g[0]]),
                 (0, LANES - N_GROUPS - N_EXPERTS)).reshape(1, LANES)
    wgu = jnp.concatenate([w_gate[0], w_up[0]], axis=-1).astype(BF16)
    wd = w_down[0].astype(BF16)

    def dense_tail(x, gla_o, diff_o, tm_post, tm_moe):
        x1, t, gates = _post(x, gla_o, diff_o, wo, g2, wr_parts, br, tm_post)
        return _moe(t, gates, x1, wgu, wd, gf, tm_moe)

    xp = x_prompt.reshape(nb * seq, D_MODEL)
    gq, gk, la, gv, gr, dqb, dk, dv, dkb, vt = _inproj(xp, g1, wt, wa2p, ba, 512, (nb, seq))
    s0 = jnp.zeros((nb, GLA_HEADS, GLA_DK, GLA_DV), F32)
    gla_o, s_prompt = _gla_prompt(gq, gk, la, gv, gr, gla_g, s0, nb, 256)
    diff_o = _attn_prompt(lams, dqb, dkb, vt, diff_g, nb, seq, lam_init)
    y_prompt = dense_tail(xp, gla_o, diff_o, 1024, 1024).reshape(nb, seq, D_MODEL)

    xs = x_sample.reshape(ns, D_MODEL)
    sgq, sgk, sla, sgv, sgr, sdqb, sdk, sdv = _inproj(xs, g1, wt, wa2p, ba, ns)
    sgla_o, s_sample = _gla_step(sgq, sgk, sla, sgv, sgr, gla_g, state_gla[0])
    sdiff_o = _attn_decode(page_table, lams, sdqb, sdk, sdv, diff_g, cache_k, cache_v, lam_init, 8)
    y_sample = dense_tail(xs, sgla_o.reshape(ns, GLA_V), sdiff_o.reshape(ns, DIFF_V), ns, ns).reshape(ns, 1, D_MODEL)

    k_prompt = dk.reshape(1, nb, seq, DIFF_HEADS, 2 * DIFF_HEAD_DIM)
    v_prompt = dv.reshape(1, nb, seq, DIFF_HEADS, DIFF_V_DIM)
    k_sample = sdk.reshape(1, ns, 1, DIFF_HEADS, 2 * DIFF_HEAD_DIM)
    v_sample = sdv.reshape(1, ns, 1, DIFF_HEADS, DIFF_V_DIM)
    return (y_prompt, y_sample, k_prompt, v_prompt, s_prompt[None], k_sample, v_sample, s_sample[None])
```
